```python
import jax, jax.numpy as jnp
from jax import lax
import numpy as np

D_MODEL = 1024
BATCH = 4
SEQ = 8192
DEPTH = 1

N_META = 16
MIX_WIDTH = D_MODEL
POOL_WIDTH = MIX_WIDTH // 2
POOL_WINDOWS = (2, 4, 8, 16)
N_POOL_GROUPS = 4
POOL_GROUP = POOL_WIDTH // N_POOL_GROUPS
GLA_HEADS = 4
GLA_VDIM = MIX_WIDTH - POOL_WIDTH
GLA_HEAD_V = GLA_VDIM // GLA_HEADS
GLA_KDIM = GLA_VDIM // 2
GLA_HEAD_K = GLA_KDIM // GLA_HEADS
GATE_RANK = 16
GATE_NORMALIZER = 16.0
CHUNK = 64
N_GROUPS = 4
EXPERTS_PER_GROUP = 4
N_EXPERTS = N_GROUPS * EXPERTS_PER_GROUP
TOP_K_IN_GROUP = 2
EXPERT_HIDDEN = D_MODEL // 4
EPS = 1e-6
IN_COLS = POOL_WIDTH + 2 * GLA_KDIM + 2 * GLA_VDIM + GATE_RANK
SPLIT_POINTS = (POOL_WIDTH,
                POOL_WIDTH + GLA_KDIM,
                POOL_WIDTH + 2 * GLA_KDIM,
                POOL_WIDTH + 2 * GLA_KDIM + GLA_VDIM,
                POOL_WIDTH + 2 * GLA_KDIM + 2 * GLA_VDIM)

kernel_name = "hymba_pool_gla_hmoe_block"


def rmsnorm(x, w):
    xf = x.astype(jnp.float32)
    y = xf * lax.rsqrt(jnp.mean(xf * xf, axis=-1, keepdims=True) + EPS)
    return (y * w.astype(jnp.float32)).astype(x.dtype)


def pool_mixer(u, pool_w, pool_scale):
    B, L, _ = u.shape
    uf = u.astype(jnp.float32).reshape(B, L, N_POOL_GROUPS, POOL_GROUP)
    cs = jnp.concatenate([jnp.zeros((B, 1, N_POOL_GROUPS, POOL_GROUP), jnp.float32),
                          jnp.cumsum(uf, axis=1)], axis=1)
    t = jnp.arange(L)
    outs = []
    for g, w in enumerate(POOL_WINDOWS):
        lo = jnp.maximum(t + 1 - w, 0)
        cs_g = cs[:, :, g]
        win_sum = cs_g[:, 1:] - cs_g[:, lo]
        cnt = (t + 1 - lo).astype(jnp.float32)[None, :, None]
        outs.append(win_sum / cnt - uf[:, :, g])
    pooled = jnp.stack(outs, axis=2)
    mixed = jnp.einsum('blgc,gcd->blgd', pooled, pool_w.astype(jnp.float32))
    y = mixed.reshape(B, L, POOL_WIDTH) * pool_scale.astype(jnp.float32)
    return y.astype(u.dtype)


def gla_mixer(q, k, v, r, g_lr, w_gate_up, b_gate, gla_norm_w):
    B, L, _ = q.shape
    f32 = jnp.float32
    log_a = jax.nn.log_sigmoid(g_lr.astype(f32) @ w_gate_up.astype(f32)
                               + b_gate.astype(f32)) / GATE_NORMALIZER
    pad = CHUNK - N_META

    def heads(a, dh):
        a = jnp.pad(a.astype(f32), ((0, 0), (pad, 0), (0, 0)))
        return a.reshape(B, -1, CHUNK, GLA_HEADS, dh).transpose(0, 3, 1, 2, 4)

    qh = heads(q, GLA_HEAD_K) * (GLA_HEAD_K ** -0.5)
    kh = heads(k, GLA_HEAD_K)
    vh = heads(v, GLA_HEAD_V)
    gh = heads(log_a, GLA_HEAD_K)
    b = jnp.cumsum(gh, axis=3)
    b_last = b[:, :, :, -1:, :]
    q_t = qh * jnp.exp(b)
    k_t = kh * jnp.exp(-b)
    k_end = kh * jnp.exp(b_last - b)
    causal = jnp.tril(jnp.ones((CHUNK, CHUNK), dtype=bool))
    scores = jnp.where(causal, jnp.einsum('bhncd,bhnsd->bhncs', q_t, k_t), 0.0)
    o_intra = jnp.einsum('bhncs,bhnsv->bhncv', scores, vh)
    dS = jnp.einsum('bhncd,bhncv->bhndv', k_end, vh)
    decay = jnp.exp(b_last[:, :, :, 0, :])

    def step(S, inp):
        dS_n, dec_n = inp
        return dec_n[..., None] * S + dS_n, S

    S0 = jnp.zeros((B, GLA_HEADS, GLA_HEAD_K, GLA_HEAD_V), f32)
    _, S_start = lax.scan(step, S0, (jnp.moveaxis(dS, 2, 0), jnp.moveaxis(decay, 2, 0)))
    S_start = jnp.moveaxis(S_start, 0, 2)
    o_inter = jnp.einsum('bhncd,bhndv->bhncv', q_t, S_start)
    o = (o_intra + o_inter).transpose(0, 2, 3, 1, 4).reshape(B, -1, GLA_HEADS, GLA_HEAD_V)[:, pad:]
    o = o * lax.rsqrt(jnp.mean(o * o, axis=-1, keepdims=True) + EPS) * gla_norm_w.astype(f32)
    o = o.reshape(B, L, GLA_VDIM) * jax.nn.silu(r.astype(f32))
    return o.astype(q.dtype)


def hier_moe(x, w_rg, b_rg, w_re, b_re, w_gate, w_up, w_down):
    B, L, D = x.shape
    f32 = jnp.float32
    xt = x.reshape(-1, D)
    xf = xt.astype(f32)
    g_logits = xf @ w_rg.astype(f32) + b_rg.astype(f32)
    g_prob = jax.nn.softmax(g_logits, axis=-1)
    g_sel = jnp.argmax(g_logits, axis=-1)
    p_g = jnp.take_along_axis(g_prob, g_sel[:, None], axis=-1)
    e_logits = (xf @ w_re.astype(f32) + b_re.astype(f32)).reshape(-1, N_GROUPS, EXPERTS_PER_GROUP)
    e_logits = jnp.take_along_axis(e_logits, g_sel[:, None, None], axis=1)[:, 0]
    e_prob = jax.nn.softmax(e_logits, axis=-1)
    top_p, top_i = lax.top_k(e_prob, TOP_K_IN_GROUP)
    top_p = top_p / jnp.sum(top_p, axis=-1, keepdims=True)
    weights = p_g * top_p
    expert_id = g_sel[:, None] * EXPERTS_PER_GROUP + top_i
    combine = jnp.sum(jax.nn.one_hot(expert_id, N_EXPERTS, dtype=f32) * weights[..., None], axis=1)
    y = jnp.zeros(xf.shape, f32)
    for e in range(N_EXPERTS):
        h = jax.nn.silu(xt @ w_gate[e]) * (xt @ w_up[e])
        y = y + combine[:, e:e + 1] * (h @ w_down[e]).astype(f32)
    return y.reshape(B, L, D).astype(x.dtype)


def setup_inputs(seed: int = 0) -> dict:
    key = jax.random.key(seed)
    ks = jax.random.split(key, 20)
    n = jax.random.normal
    D = D_MODEL
    return {
        "x": n(ks[0], (BATCH, SEQ, D), jnp.float32),
        "meta_tokens": n(ks[1], (N_META, D), jnp.float32),
        "norm_mix_w": 1.0 + 0.05 * n(ks[2], (DEPTH, D), jnp.float32),
        "w_in": n(ks[3], (DEPTH, D, IN_COLS), jnp.float32) * D ** -0.5,
        "w_gate_up": n(ks[4], (DEPTH, GATE_RANK, GLA_KDIM), jnp.float32) * GATE_RANK ** -0.5,
        "b_gate": 0.1 * n(ks[5], (DEPTH, GLA_KDIM), jnp.float32),
        "gla_norm_w": 1.0 + 0.05 * n(ks[6], (DEPTH, GLA_HEAD_V), jnp.float32),
        "pool_w": n(ks[7], (DEPTH, N_POOL_GROUPS, POOL_GROUP, POOL_GROUP), jnp.float32) * POOL_GROUP ** -0.5,
        "pool_scale": 1.0 + 0.05 * n(ks[8], (DEPTH, POOL_WIDTH), jnp.float32),
        "w_out": n(ks[9], (DEPTH, MIX_WIDTH, D), jnp.float32) * MIX_WIDTH ** -0.5,
        "norm_ffn_w": 1.0 + 0.05 * n(ks[10], (DEPTH, D), jnp.float32),
        "w_router_group": n(ks[11], (DEPTH, D, N_GROUPS), jnp.float32) * D ** -0.5,
        "b_router_group": 0.01 * n(ks[12], (DEPTH, N_GROUPS), jnp.float32),
        "w_router_expert": n(ks[13], (DEPTH, D, N_EXPERTS), jnp.float32) * D ** -0.5,
        "b_router_expert": 0.01 * n(ks[14], (DEPTH, N_EXPERTS), jnp.float32),
        "w_expert_gate": n(ks[15], (DEPTH, N_EXPERTS, D, EXPERT_HIDDEN), jnp.float32) * D ** -0.5,
        "w_expert_up": n(ks[16], (DEPTH, N_EXPERTS, D, EXPERT_HIDDEN), jnp.float32) * D ** -0.5,
        "w_expert_down": n(ks[17], (DEPTH, N_EXPERTS, EXPERT_HIDDEN, D), jnp.float32) * EXPERT_HIDDEN ** -0.5,
        "final_norm_w": 1.0 + 0.05 * n(ks[18], (D,), jnp.float32),
    }


def reference(x, meta_tokens, norm_mix_w, w_in, w_gate_up, b_gate, gla_norm_w, pool_w,
              pool_scale, w_out, norm_ffn_w, w_router_group, b_router_group,
              w_router_expert, b_router_expert, w_expert_gate, w_expert_up,
              w_expert_down, final_norm_w):
    B = x.shape[0]
    meta = jnp.broadcast_to(meta_tokens[None].astype(x.dtype), (B, N_META, D_MODEL))
    h = jnp.concatenate([meta, x], axis=1)
    for l in range(DEPTH):
        u = rmsnorm(h, norm_mix_w[l])
        proj = u @ w_in[l]
        pool_v, q, k, v, r, g_lr = jnp.split(proj, SPLIT_POINTS, axis=-1)
        y_pool = pool_mixer(pool_v, pool_w[l], pool_scale[l])
        y_gla = gla_mixer(q, k, v, r, g_lr, w_gate_up[l], b_gate[l], gla_norm_w[l])
        h = h + jnp.concatenate([y_pool, y_gla], axis=-1) @ w_out[l]
        h = h + hier_moe(rmsnorm(h, norm_ffn_w[l]), w_router_group[l], b_router_group[l],
                         w_router_expert[l], b_router_expert[l], w_expert_gate[l],
                         w_expert_up[l], w_expert_down[l])
    return rmsnorm(h, final_norm_w)[:, N_META:]
```

```python
import functools

import jax
import jax.numpy as jnp
from jax import lax
from jax.experimental import pallas as pl
from jax.experimental.pallas import tpu as pltpu

F32 = jnp.float32
BF16 = jnp.bfloat16
I32 = jnp.int32

D_MODEL = 1024
N_META = 16
POOL_WIDTH = 512
POOL_WINDOWS = (2, 4, 8, 16)
POOL_GROUP = 128
GLA_HEADS = 4
GLA_VDIM = 512
GLA_HEAD_V = 128
GLA_KDIM = 256
GLA_HEAD_K = 64
GATE_RANK = 16
GATE_NORMALIZER = 16.0
CHUNK = 64
N_GROUPS = 4
EXPERTS_PER_GROUP = 4
N_EXPERTS = 16
EXPERT_HIDDEN = 256
EPS = 1e-6

LANES = 128
IN_COLS = POOL_WIDTH + 2 * GLA_KDIM + 2 * GLA_VDIM + GATE_RANK
IN_PAD = POOL_WIDTH + 2 * GLA_KDIM + 2 * GLA_VDIM + LANES
C_Q = POOL_WIDTH
C_K = C_Q + GLA_KDIM
C_V = C_K + GLA_KDIM
C_R = C_V + GLA_VDIM
C_G = C_R + GLA_VDIM
HEXT = D_MODEL + LANES

TN = 256
N_CHUNKS = TN // CHUNK
N_PAIRS = GLA_HEADS // 2
HIST = 16

TM = 256
N_PAIR_BUCKETS = 6
N_BUCKETS = N_GROUPS * N_PAIR_BUCKETS
VMEM_LIMIT = 48 * 1024 * 1024


def _dot(a, b):
    return jnp.dot(a, b, preferred_element_type=F32)


def _dot_nt(a, b):
    return lax.dot_general(a, b, (((1,), (1,)), ((), ())), preferred_element_type=F32)


def _dot_tn(a, b):
    return lax.dot_general(a, b, (((0,), (0,)), ((), ())), preferred_element_type=F32)


def _rms(x, w):
    return x * lax.rsqrt(jnp.mean(x * x, axis=-1, keepdims=True) + EPS) * w


def _log_sigmoid(z):
    return -(jnp.maximum(-z, 0.0) + jnp.log1p(jnp.exp(-jnp.abs(z))))


def _seg_cumsum(x, seg):
    rows = x.shape[0]
    pos = lax.broadcasted_iota(I32, x.shape, 0) % seg
    s = 1
    while s < seg:
        shifted = pltpu.roll(x, s, axis=0)
        x = x + jnp.where(pos >= s, shifted, 0.0)
        s *= 2
    del rows
    return x


def _log_decay(g, wgu_ref, bg_ref):
    z = _dot(g.astype(BF16), wgu_ref[...]) + bg_ref[...]
    return _log_sigmoid(z) / GATE_NORMALIZER


def _state_mask():
    r = lax.broadcasted_iota(I32, (2 * GLA_HEAD_V, 2 * GLA_HEAD_K), 0) // GLA_HEAD_V
    c = lax.broadcasted_iota(I32, (2 * GLA_HEAD_V, 2 * GLA_HEAD_K), 1) // GLA_HEAD_K
    return r == c


def _mixer_kernel(x_ref, meta_ref, nmw_ref, win_ref, wgu_ref, bg_ref, gnw_ref, pw_ref, ps_ref,
                  wout_ref, nfw_ref, wrh_ref, wrl_ref, br_ref,
                  hext_ref, route_ref, pvh_ref, st_ref):
    t = pl.program_id(1)
    smask = _state_mask()

    @pl.when(t == 0)
    def _():
        um = _rms(meta_ref[...], nmw_ref[...]).astype(BF16)
        pm = _dot(um, win_ref[...])
        pvh_ref[0:HIST, :] = pm[:, 0:POOL_WIDTH]
        la = _log_decay(pm[:, C_G:IN_PAD], wgu_ref, bg_ref)
        b = _seg_cumsum(la, N_META)
        k_end = (pm[:, C_K:C_V] * jnp.exp(b[N_META - 1:N_META, :] - b)).astype(BF16)
        vm = pm[:, C_V:C_R].astype(BF16)
        for p in range(N_PAIRS):
            ds = _dot_tn(vm[:, 2 * GLA_HEAD_V * p:2 * GLA_HEAD_V * (p + 1)],
                         k_end[:, LANES * p:LANES * (p + 1)])
            st_ref[p] = jnp.where(smask, ds, 0.0)

    x = x_ref[...]
    u = _rms(x, nmw_ref[...]).astype(BF16)
    proj = _dot(u, win_ref[...])

    pv = proj[:, 0:POOL_WIDTH]
    pvh_ref[HIST:HIST + TN, :] = pv
    mixed = []
    for gi, w in enumerate(POOL_WINDOWS):
        cs = slice(POOL_GROUP * gi, POOL_GROUP * (gi + 1))
        acc = pvh_ref[pl.ds(HIST, TN), cs]
        for j in range(1, w):
            acc = acc + pvh_ref[pl.ds(HIST - j, TN), cs]
        pooled = acc * (1.0 / w) - pv[:, cs]
        mixed.append(_dot(pooled.astype(BF16), pw_ref[gi]))
    y_pool = jnp.concatenate(mixed, axis=1) * ps_ref[...]
    pvh_ref[0:HIST, :] = pvh_ref[TN:TN + HIST, :]

    q = proj[:, C_Q:C_K]
    k = proj[:, C_K:C_V]
    v = proj[:, C_V:C_R].astype(BF16)
    r = proj[:, C_R:C_G]
    la = _log_decay(proj[:, C_G:IN_PAD], wgu_ref, bg_ref)
    b = _seg_cumsum(la, CHUNK)
    b3 = b.reshape(N_CHUNKS, CHUNK, GLA_KDIM)
    bl = b3[:, CHUNK - 1:CHUNK, :]
    blb = jnp.broadcast_to(bl, (N_CHUNKS, CHUNK, GLA_KDIM)).reshape(TN, GLA_KDIM)
    q_t = ((q * (GLA_HEAD_K ** -0.5)) * jnp.exp(b)).astype(BF16)
    k_t = (k * jnp.exp(-b)).astype(BF16)
    k_end = (k * jnp.exp(blb - b)).astype(BF16)
    dec = jnp.exp(bl)

    ri = lax.broadcasted_iota(I32, (TN, TN), 0)
    ci = lax.broadcasted_iota(I32, (TN, TN), 1)
    causal = (ri // CHUNK == ci // CHUNK) & (ci <= ri)
    lane_k = lax.broadcasted_iota(I32, (TN, LANES), 1) // GLA_HEAD_K

    o_heads = [None] * GLA_HEADS
    for p in range(N_PAIRS):
        ks = slice(LANES * p, LANES * (p + 1))
        vs = slice(2 * GLA_HEAD_V * p, 2 * GLA_HEAD_V * (p + 1))
        qp = q_t[:, ks]
        kp = k_t[:, ks]
        o_inter = []
        st = st_ref[p]
        for c in range(N_CHUNKS):
            rs = slice(CHUNK * c, CHUNK * (c + 1))
            o_inter.append(_dot_nt(qp[rs], st.astype(BF16)))
            ds = _dot_tn(v[rs, vs], k_end[rs, ks])
            st = st * dec[c][:, ks] + jnp.where(smask, ds, 0.0)
        st_ref[p] = st
        o_inter = jnp.concatenate(o_inter, axis=0)
        for hh in range(2):
            head = 2 * p + hh
            qm = jnp.where(lane_k == hh, qp, jnp.zeros_like(qp))
            sc = jnp.where(causal, _dot_nt(qm, kp), 0.0).astype(BF16)
            hv = slice(GLA_HEAD_V * head, GLA_HEAD_V * (head + 1))
            o_heads[head] = _dot(sc, v[:, hv]) + o_inter[:, GLA_HEAD_V * hh:GLA_HEAD_V * (hh + 1)]

    gnw = gnw_ref[...]
    y_heads = []
    for head in range(GLA_HEADS):
        o = o_heads[head]
        o = o * lax.rsqrt(jnp.mean(o * o, axis=-1, keepdims=True) + EPS) * gnw
        y_heads.append(o)
    y_gla = jnp.concatenate(y_heads, axis=1) * (r * jax.nn.sigmoid(r))

    ycat = jnp.concatenate([y_pool, y_gla], axis=1).astype(BF16)
    h = x + _dot(ycat, wout_ref[...])

    xn = _rms(h, nfw_ref[...])
    xh = xn.astype(BF16)
    xl = (xn - xh.astype(F32)).astype(BF16)
    logits = (_dot(xh, wrh_ref[...]) + _dot(xh, wrl_ref[...]) + _dot(xl, wrh_ref[...])) + br_ref[...]

    lane = lax.broadcasted_iota(I32, (TN, LANES), 1)
    lanef = lane.astype(F32)
    neg = -jnp.inf
    big = 1e9
    is_g = lane < N_GROUPS
    gl = jnp.where(is_g, logits, neg)
    gmax = jnp.max(gl, axis=-1, keepdims=True)
    gsel = jnp.min(jnp.where(gl == gmax, lanef, big), axis=-1, keepdims=True)
    gsum = jnp.sum(jnp.where(is_g, jnp.exp(logits - gmax), 0.0), axis=-1, keepdims=True)
    p_g = 1.0 / gsum
    e_lo = N_GROUPS + EXPERTS_PER_GROUP * gsel
    is_e = (lanef >= e_lo) & (lanef < e_lo + EXPERTS_PER_GROUP)
    el = jnp.where(is_e, logits, neg)
    emax = jnp.max(el, axis=-1, keepdims=True)
    ee = jnp.where(is_e, jnp.exp(logits - emax), 0.0)
    ep = ee / jnp.sum(ee, axis=-1, keepdims=True)
    p1 = jnp.max(jnp.where(is_e, ep, -1.0), axis=-1, keepdims=True)
    i1 = jnp.min(jnp.where(is_e, jnp.where(ep == p1, lanef, big), big), axis=-1, keepdims=True)
    rest = is_e & (lanef != i1)
    p2 = jnp.max(jnp.where(rest, ep, -1.0), axis=-1, keepdims=True)
    i2 = jnp.min(jnp.where(rest, jnp.where(ep == p2, lanef, big), big), axis=-1, keepdims=True)
    den = p1 + p2
    w1 = p_g * (p1 / den)
    w2 = p_g * (p2 / den)
    j1 = i1 - e_lo
    j2 = i2 - e_lo
    first_low = j1 < j2
    ja = jnp.where(first_low, j1, j2)
    jb = jnp.where(first_low, j2, j1)
    wa = jnp.where(first_low, w1, w2)
    wb = jnp.where(first_low, w2, w1)

    route = jnp.where(lane == 0, gsel,
            jnp.where(lane == 1, ja,
            jnp.where(lane == 2, jb,
            jnp.where(lane == 3, wa,
            jnp.where(lane == 4, wb, 0.0)))))
    route_ref[...] = route[:, 0:8]
    hext_ref[:, 0:D_MODEL] = h
    hext_ref[:, D_MODEL:HEXT] = jnp.where(lane == 0, wa, jnp.where(lane == 1, wb, 0.0))


def _mixer_call(x, meta, nmw, win, wgu, bg, gnw, pw, ps, wout, nfw, wrh, wrl, br):
    bsz, seq, _ = x.shape
    full = lambda shape: pl.BlockSpec(shape, lambda b, t: (0,) * len(shape))
    return pl.pallas_call(
        _mixer_kernel,
        grid=(bsz, seq // TN),
        in_specs=[
            pl.BlockSpec((None, TN, D_MODEL), lambda b, t: (b, t, 0)),
            full((N_META, D_MODEL)),
            full((1, D_MODEL)),
            full((D_MODEL, IN_PAD)),
            full((LANES, GLA_KDIM)),
            full((1, GLA_KDIM)),
            full((1, GLA_HEAD_V)),
            full((len(POOL_WINDOWS), POOL_GROUP, POOL_GROUP)),
            full((1, POOL_WIDTH)),
            full((D_MODEL, D_MODEL)),
            full((1, D_MODEL)),
            full((D_MODEL, LANES)),
            full((D_MODEL, LANES)),
            full((1, LANES)),
        ],
        out_specs=[
            pl.BlockSpec((None, TN, HEXT), lambda b, t: (b, t, 0)),
            pl.BlockSpec((None, TN, 8), lambda b, t: (b, t, 0)),
        ],
        out_shape=[
            jax.ShapeDtypeStruct((bsz, seq, HEXT), F32),
            jax.ShapeDtypeStruct((bsz, seq, 8), F32),
        ],
        scratch_shapes=[
            pltpu.VMEM((HIST + TN, POOL_WIDTH), F32),
            pltpu.VMEM((N_PAIRS, 2 * GLA_HEAD_V, 2 * GLA_HEAD_K), F32),
        ],
        compiler_params=pltpu.CompilerParams(
            dimension_semantics=("arbitrary", "arbitrary"),
            vmem_limit_bytes=VMEM_LIMIT),
        name="mixer",
    )(x, meta, nmw, win, wgu, bg, gnw, pw, ps, wout, nfw, wrh, wrl, br)


def _row_gather_copy(hext_hbm, hbuf, sem, slot, src_row, dst_row):
    return pltpu.make_async_copy(hext_hbm.at[pl.ds(src_row, 1), :],
                                 hbuf.at[slot, pl.ds(dst_row, 1), :], sem.at[slot])


def _row_scatter_copy(obuf, out_hbm, sem, slot, src_row, dst_row):
    return pltpu.make_async_copy(obuf.at[slot, pl.ds(src_row, 1), :],
                                 out_hbm.at[pl.ds(dst_row, 1), :], sem.at[slot])


def _issue_gathers(idx_ref, hext_hbm, hbuf, gsem, slot):
    def body(rr, carry):
        src = jnp.maximum(idx_ref[0, rr], 0)
        _row_gather_copy(hext_hbm, hbuf, gsem, slot, src, rr).start()
        return carry
    lax.fori_loop(0, TM, body, 0, unroll=8)


def _wait_scatters(obuf, out_hbm, ssem, slot, nrows):
    size = TM
    while size >= 1:
        @pl.when((nrows & size) != 0)
        def _(size=size):
            pltpu.make_async_copy(obuf.at[slot, pl.ds(0, size), :],
                                  out_hbm.at[pl.ds(0, size), :], ssem.at[slot]).wait()
        size //= 2


def _moe_kernel(ea_ref, eb_ref, nv_ref, na_ref,
                idx_ref, idxn_ref, hext_hbm, wgua_ref, wgub_ref, wda_ref, wdb_ref, nfw_ref, fnw_ref,
                out_hbm, hbuf, obuf, gsem, ssem):
    del ea_ref, eb_ref
    i = pl.program_id(0)
    n_act = na_ref[0]
    slot = i % 2

    @pl.when(i == 0)
    def _():
        _issue_gathers(idx_ref, hext_hbm, hbuf, gsem, 0)

    @pl.when(i + 1 < n_act)
    def _():
        _issue_gathers(idxn_ref, hext_hbm, hbuf, gsem, 1 - slot)

    @pl.when(i < n_act)
    def _():
        pltpu.make_async_copy(hext_hbm.at[pl.ds(0, TM), :], hbuf.at[slot], gsem.at[slot]).wait()
        rows = hbuf[slot]
        hrow = rows[:, 0:D_MODEL]
        wa = rows[:, D_MODEL:D_MODEL + 1]
        wb = rows[:, D_MODEL + 1:D_MODEL + 2]
        xn = _rms(hrow, nfw_ref[...]).astype(BF16)

        def expert(wgu_ref, wd_ref):
            gu = _dot(xn, wgu_ref[...])
            gate = gu[:, 0:EXPERT_HIDDEN]
            hid = (gate * jax.nn.sigmoid(gate)) * gu[:, EXPERT_HIDDEN:2 * EXPERT_HIDDEN]
            return _dot(hid.astype(BF16), wd_ref[...])

        y = wa * expert(wgua_ref, wda_ref)
        y = y + wb * expert(wgub_ref, wdb_ref)
        res = _rms(hrow + y, fnw_ref[...])

        @pl.when(i >= 2)
        def _():
            _wait_scatters(obuf, out_hbm, ssem, slot, nv_ref[jnp.maximum(i - 2, 0)])

        obuf[slot] = res
        nv = nv_ref[i]

        def body(rr, carry):
            _row_scatter_copy(obuf, out_hbm, ssem, slot, rr, idx_ref[0, rr]).start()
            return carry
        lax.fori_loop(0, nv, body, 0)

    @pl.when(i == n_act - 1)
    def _():
        _wait_scatters(obuf, out_hbm, ssem, slot, nv_ref[i])

        @pl.when(i >= 1)
        def _():
            _wait_scatters(obuf, out_hbm, ssem, 1 - slot, nv_ref[jnp.maximum(i - 1, 0)])


def _moe_call(ea, eb, nvalid, nact, idx, hext, wgu, wd, nfw, fnw, n_tokens):
    n_tiles = idx.shape[0]
    grid_spec = pltpu.PrefetchScalarGridSpec(
        num_scalar_prefetch=4,
        grid=(n_tiles,),
        in_specs=[
            pl.BlockSpec((None, 1, TM), lambda i, ea, eb, nv, na: (i, 0, 0),
                         memory_space=pltpu.SMEM),
            pl.BlockSpec((None, 1, TM), lambda i, ea, eb, nv, na: (jnp.minimum(i + 1, n_tiles - 1), 0, 0),
                         memory_space=pltpu.SMEM),
            pl.BlockSpec(memory_space=pl.ANY),
            pl.BlockSpec((None, D_MODEL, 2 * EXPERT_HIDDEN), lambda i, ea, eb, nv, na: (ea[i], 0, 0)),
            pl.BlockSpec((None, D_MODEL, 2 * EXPERT_HIDDEN), lambda i, ea, eb, nv, na: (eb[i], 0, 0)),
            pl.BlockSpec((None, EXPERT_HIDDEN, D_MODEL), lambda i, ea, eb, nv, na: (ea[i], 0, 0)),
            pl.BlockSpec((None, EXPERT_HIDDEN, D_MODEL), lambda i, ea, eb, nv, na: (eb[i], 0, 0)),
            pl.BlockSpec((1, D_MODEL), lambda i, ea, eb, nv, na: (0, 0)),
            pl.BlockSpec((1, D_MODEL), lambda i, ea, eb, nv, na: (0, 0)),
        ],
        out_specs=pl.BlockSpec(memory_space=pl.ANY),
        scratch_shapes=[
            pltpu.VMEM((2, TM, HEXT), F32),
            pltpu.VMEM((2, TM, D_MODEL), F32),
            pltpu.SemaphoreType.DMA((2,)),
            pltpu.SemaphoreType.DMA((2,)),
        ],
    )
    return pl.pallas_call(
        _moe_kernel,
        grid_spec=grid_spec,
        out_shape=jax.ShapeDtypeStruct((n_tokens, D_MODEL), F32),
        compiler_params=pltpu.CompilerParams(
            dimension_semantics=("arbitrary",),
            vmem_limit_bytes=VMEM_LIMIT),
        name="moe",
    )(ea, eb, nvalid, nact, idx, idx, hext, wgu, wgu, wd, wd, nfw, fnw)


def _slot_tables(route, n_tokens):
    g = route[:, 0].astype(I32)
    ja = route[:, 1].astype(I32)
    jb = route[:, 2].astype(I32)
    pair = (ja * (7 - ja)) // 2 + (jb - ja - 1)
    bucket = g * N_PAIR_BUCKETS + pair
    onehot = (bucket[:, None] == jnp.arange(N_BUCKETS, dtype=I32)[None, :]).astype(I32)
    csum = jnp.cumsum(onehot, axis=0)
    counts = csum[-1]
    rank = jnp.sum(csum * onehot, axis=1) - 1
    padded = ((counts + TM - 1) // TM) * TM
    ends = jnp.cumsum(padded)
    starts = ends - padded
    dst = jnp.take(starts, bucket) + rank
    n_tiles = n_tokens // TM + N_BUCKETS
    src = jnp.full((n_tiles * TM,), -1, I32).at[dst].set(
        jnp.arange(n_tokens, dtype=I32), unique_indices=True)
    tile_start = jnp.arange(n_tiles, dtype=I32) * TM
    total = ends[-1]
    nact = total // TM
    tb = jnp.minimum(jnp.sum((tile_start[:, None] >= ends[None, :]).astype(I32), axis=1), N_BUCKETS - 1)
    valid = tile_start < total
    tb = jnp.where(valid, tb, jnp.take(tb, jnp.maximum(nact - 1, 0)))
    nvalid = jnp.where(valid, jnp.clip(jnp.take(starts, tb) + jnp.take(counts, tb) - tile_start, 0, TM), 0)
    pair_a = jnp.array([0, 0, 0, 1, 1, 2], I32)
    pair_b = jnp.array([1, 2, 3, 2, 3, 3], I32)
    ea = (tb // N_PAIR_BUCKETS) * EXPERTS_PER_GROUP + jnp.take(pair_a, tb % N_PAIR_BUCKETS)
    eb = (tb // N_PAIR_BUCKETS) * EXPERTS_PER_GROUP + jnp.take(pair_b, tb % N_PAIR_BUCKETS)
    return ea, eb, nvalid.astype(I32), jnp.reshape(nact, (1,)).astype(I32), src.reshape(n_tiles, 1, TM)


def kernel(x, meta_tokens, norm_mix_w, w_in, w_gate_up, b_gate, gla_norm_w, pool_w, pool_scale, w_out, norm_ffn_w, w_router_group, b_router_group, w_router_expert, b_router_expert, w_expert_gate, w_expert_up, w_expert_down, final_norm_w):
    bsz, seq, d = x.shape
    assert d == D_MODEL and seq % TN == 0 and (bsz * seq) % TM == 0
    assert norm_mix_w.shape[0] == 1
    n_tokens = bsz * seq

    win = jnp.pad(w_in[0], ((0, 0), (0, IN_PAD - IN_COLS))).astype(BF16)
    wgu_gate = jnp.pad(w_gate_up[0], ((0, LANES - GATE_RANK), (0, 0))).astype(BF16)
    w_r = jnp.pad(jnp.concatenate([w_router_group[0], w_router_expert[0]], axis=1),
                  ((0, 0), (0, LANES - N_GROUPS - N_EXPERTS)))
    w_r_hi = w_r.astype(BF16)
    w_r_lo = (w_r - w_r_hi.astype(F32)).astype(BF16)
    b_r = jnp.pad(jnp.concatenate([b_router_group[0], b_router_expert[0]]),
                  (0, LANES - N_GROUPS - N_EXPERTS)).reshape(1, LANES)

    hext, route = _mixer_call(
        x, meta_tokens, norm_mix_w[0].reshape(1, d), win, wgu_gate, b_gate[0].reshape(1, GLA_KDIM),
        gla_norm_w[0].reshape(1, GLA_HEAD_V), pool_w[0].astype(BF16), pool_scale[0].reshape(1, POOL_WIDTH),
        w_out[0].astype(BF16), norm_ffn_w[0].reshape(1, d), w_r_hi, w_r_lo, b_r)

    ea, eb, nvalid, nact, idx = _slot_tables(route.reshape(n_tokens, 8), n_tokens)
    wgu = jnp.concatenate([w_expert_gate[0], w_expert_up[0]], axis=-1).astype(BF16)
    wd = w_expert_down[0].astype(BF16)
    out = _moe_call(ea, eb, nvalid, nact, idx, hext.reshape(n_tokens, HEXT), wgu, wd,
                    norm_ffn_w[0].reshape(1, d), final_norm_w.reshape(1, d), n_tokens)
    return out.reshape(bsz, seq, d)
```

```python
import jax
import jax.numpy as jnp
from jax import lax
from jax.experimental import pallas as pl
from jax.experimental.pallas import tpu as pltpu

F32 = jnp.float32
BF16 = jnp.bfloat16
I32 = jnp.int32

D_MODEL = 1024
N_META = 16
POOL_WIDTH = 512
POOL_WINDOWS = (2, 4, 8, 16)
POOL_GROUP = 128
GLA_HEADS = 4
GLA_VDIM = 512
GLA_HEAD_V = 128
GLA_KDIM = 256
GLA_HEAD_K = 64
GATE_RANK = 16
GATE_NORMALIZER = 16.0
CHUNK = 64
N_GROUPS = 4
EXPERTS_PER_GROUP = 4
N_EXPERTS = 16
EXPERT_HIDDEN = 256
EPS = 1e-6

LANES = 128
IN_COLS = POOL_WIDTH + 2 * GLA_KDIM + 2 * GLA_VDIM + GATE_RANK
IN_PAD = POOL_WIDTH + 2 * GLA_KDIM + 2 * GLA_VDIM + LANES
C_Q = POOL_WIDTH
C_K = C_Q + GLA_KDIM
C_V = C_K + GLA_KDIM
C_R = C_V + GLA_VDIM
C_G = C_R + GLA_VDIM
HEXT = D_MODEL + LANES
ROUTE_ROWS = 32

TN = 256
SEQS = 2
N_CHUNKS = TN // CHUNK
N_PAIRS = GLA_HEADS // 2
HIST = 16

TM = 256
N_PAIR_BUCKETS = 6
N_BUCKETS = N_GROUPS * N_PAIR_BUCKETS
VMEM_LIMIT = 56 * 1024 * 1024


def _dot(a, b):
    return jnp.dot(a, b, preferred_element_type=F32)


def _dot_nt(a, b):
    return lax.dot_general(a, b, (((1,), (1,)), ((), ())), preferred_element_type=F32)


def _dot_tn(a, b):
    return lax.dot_general(a, b, (((0,), (0,)), ((), ())), preferred_element_type=F32)


def _rms(x, w):
    return x * lax.rsqrt(jnp.mean(x * x, axis=-1, keepdims=True) + EPS) * w


def _log_sigmoid(z):
    return -(jnp.maximum(-z, 0.0) + jnp.log1p(jnp.exp(-jnp.abs(z))))


def _seg_cumsum(x, seg):
    pos = lax.broadcasted_iota(I32, x.shape, 0) % seg
    s = 1
    while s < seg:
        x = x + jnp.where(pos >= s, pltpu.roll(x, s, axis=0), 0.0)
        s *= 2
    return x


def _log_decay(g, wgu_ref, bg_ref):
    z = _dot(g.astype(BF16), wgu_ref[...]) + bg_ref[...]
    return _log_sigmoid(z) / GATE_NORMALIZER


def _state_mask():
    r = lax.broadcasted_iota(I32, (2 * GLA_HEAD_V, 2 * GLA_HEAD_K), 0) // GLA_HEAD_V
    c = lax.broadcasted_iota(I32, (2 * GLA_HEAD_V, 2 * GLA_HEAD_K), 1) // GLA_HEAD_K
    return r == c


def _first_index(vals, target):
    idx = jnp.full_like(target, float(len(vals) - 1))
    for j in range(len(vals) - 2, -1, -1):
        idx = jnp.where(vals[j] == target, float(j), idx)
    return idx


def _max_list(vals):
    m = vals[0]
    for val in vals[1:]:
        m = jnp.maximum(m, val)
    return m


def _route(lt):
    row = lambda i: lt[i:i + 1, :]
    gl = [row(i) for i in range(N_GROUPS)]
    gmax = _max_list(gl)
    gsel = _first_index(gl, gmax)
    gsum = gl[0] * 0.0
    for val in gl:
        gsum = gsum + jnp.exp(val - gmax)
    p_g = 1.0 / gsum
    el = []
    for j in range(EXPERTS_PER_GROUP):
        val = row(N_GROUPS + EXPERTS_PER_GROUP * (N_GROUPS - 1) + j)
        for g in range(N_GROUPS - 2, -1, -1):
            val = jnp.where(gsel == float(g), row(N_GROUPS + EXPERTS_PER_GROUP * g + j), val)
        el.append(val)
    emax = _max_list(el)
    ee = [jnp.exp(val - emax) for val in el]
    esum = ee[0]
    for val in ee[1:]:
        esum = esum + val
    ep = [val / esum for val in ee]
    p1 = _max_list(ep)
    j1 = _first_index(ep, p1)
    ep_rest = [jnp.where(j1 == float(j), -1.0, ep[j]) for j in range(EXPERTS_PER_GROUP)]
    p2 = _max_list(ep_rest)
    j2 = _first_index(ep_rest, p2)
    den = p1 + p2
    w1 = p_g * (p1 / den)
    w2 = p_g * (p2 / den)
    first_low = j1 < j2
    ja = jnp.where(first_low, j1, j2)
    jb = jnp.where(first_low, j2, j1)
    wa = jnp.where(first_low, w1, w2)
    wb = jnp.where(first_low, w2, w1)
    zero = jnp.zeros_like(wa)
    return jnp.concatenate([gsel, ja, jb, wa, wb, zero, zero, zero], axis=0)


def _mixer_tile(x, pvh_ref, st_ref, nmw_ref, win_ref, wgu_ref, bg_ref, gnw_ref, pw_ref, ps_ref,
                wout_ref, nfw_ref, wr_ref, br_ref, h_out, route_out, wtile_out):
    smask = _state_mask()
    u = _rms(x, nmw_ref[...]).astype(BF16)
    proj = _dot(u, win_ref[...])
    yield

    pv = proj[:, 0:POOL_WIDTH]
    pvh_ref[HIST:HIST + TN, :] = pv
    ext = pvh_ref[...]
    sums = [ext[:, POOL_GROUP * gi:POOL_GROUP * (gi + 1)] for gi in range(len(POOL_WINDOWS))]
    shift = 1
    for stage in range(len(POOL_WINDOWS)):
        for gi in range(stage, len(POOL_WINDOWS)):
            sums[gi] = sums[gi] + pltpu.roll(sums[gi], shift, axis=0)
        shift *= 2
    mixed = []
    for gi, w in enumerate(POOL_WINDOWS):
        cs = slice(POOL_GROUP * gi, POOL_GROUP * (gi + 1))
        pooled = sums[gi][HIST:HIST + TN, :] * (1.0 / w) - pv[:, cs]
        mixed.append(_dot(pooled.astype(BF16), pw_ref[gi]))
    y_pool = jnp.concatenate(mixed, axis=1) * ps_ref[...]
    pvh_ref[0:HIST, :] = ext[TN:TN + HIST, :]

    q = proj[:, C_Q:C_K]
    k = proj[:, C_K:C_V]
    v = proj[:, C_V:C_R].astype(BF16)
    r = proj[:, C_R:C_G]
    la = _log_decay(proj[:, C_G:IN_PAD], wgu_ref, bg_ref)
    b = _seg_cumsum(la, CHUNK)
    b3 = b.reshape(N_CHUNKS, CHUNK, GLA_KDIM)
    bl = b3[:, CHUNK - 1:CHUNK, :]
    blb = jnp.broadcast_to(bl, (N_CHUNKS, CHUNK, GLA_KDIM)).reshape(TN, GLA_KDIM)
    q_t = ((q * (GLA_HEAD_K ** -0.5)) * jnp.exp(b)).astype(BF16)
    k_t = (k * jnp.exp(-b)).astype(BF16)
    k_end = (k * jnp.exp(blb - b)).astype(BF16)
    dec = jnp.exp(bl)
    yield

    ri = lax.broadcasted_iota(I32, (TN, TN), 0)
    ci = lax.broadcasted_iota(I32, (TN, TN), 1)
    causal = (ri // CHUNK == ci // CHUNK) & (ci <= ri)
    lane_k = lax.broadcasted_iota(I32, (TN, LANES), 1) // GLA_HEAD_K

    o_heads = [None] * GLA_HEADS
    for p in range(N_PAIRS):
        ks = slice(LANES * p, LANES * (p + 1))
        vs = slice(2 * GLA_HEAD_V * p, 2 * GLA_HEAD_V * (p + 1))
        qp = q_t[:, ks]
        kp = k_t[:, ks]
        o_inter = []
        st = st_ref[p]
        for c in range(N_CHUNKS):
            rs = slice(CHUNK * c, CHUNK * (c + 1))
            o_inter.append(_dot_nt(qp[rs], st.astype(BF16)))
            ds = _dot_tn(v[rs, vs], k_end[rs, ks])
            st = st * dec[c][:, ks] + jnp.where(smask, ds, 0.0)
        st_ref[p] = st
        o_inter = jnp.concatenate(o_inter, axis=0)
        for hh in range(2):
            head = 2 * p + hh
            qm = jnp.where(lane_k == hh, qp, jnp.zeros_like(qp))
            sc = jnp.where(causal, _dot_nt(qm, kp), 0.0).astype(BF16)
            hv = slice(GLA_HEAD_V * head, GLA_HEAD_V * (head + 1))
            o_heads[head] = _dot(sc, v[:, hv]) + o_inter[:, GLA_HEAD_V * hh:GLA_HEAD_V * (hh + 1)]

    gnw = gnw_ref[...]
    y_heads = []
    for head in range(GLA_HEADS):
        o = o_heads[head]
        y_heads.append(o * lax.rsqrt(jnp.mean(o * o, axis=-1, keepdims=True) + EPS) * gnw)
    y_gla = jnp.concatenate(y_heads, axis=1) * (r * jax.nn.sigmoid(r))
    yield

    ycat = jnp.concatenate([y_pool, y_gla], axis=1).astype(BF16)
    h = x + _dot(ycat, wout_ref[...])
    h_out[...] = h
    yield

    xn = _rms(h, nfw_ref[...])
    xh = xn.astype(BF16)
    xl = (xn - xh.astype(F32)).astype(BF16)
    ph = _dot_nt(wr_ref[...], xh)
    plo = _dot_nt(wr_ref[...], xl)
    lt = ((ph[0:ROUTE_ROWS] + ph[ROUTE_ROWS:2 * ROUTE_ROWS])
          + (plo[0:ROUTE_ROWS] + plo[ROUTE_ROWS:2 * ROUTE_ROWS])) + br_ref[...]
    route = _route(lt)
    route_out[...] = route
    wtile_out[...] = jnp.concatenate([route, jnp.zeros((LANES - 8, TN), F32)], axis=0).T


def _mixer_kernel(x_ref, meta_ref, nmw_ref, winf_ref, wgu_ref, bg_ref, gnw_ref, pw_ref, ps_ref,
                  woutf_ref, nfw_ref, wr_ref, br_ref,
                  hext_ref, route_ref, win_ref, wout_ref, pvh_ref, st_ref):
    bi = pl.program_id(0)
    t = pl.program_id(1)

    @pl.when((bi == 0) & (t == 0))
    def _():
        win_ref[:, 0:C_G] = winf_ref[:, 0:C_G].astype(BF16)
        win_ref[:, C_G:IN_PAD] = jnp.zeros((D_MODEL, LANES), BF16)
        win_ref[:, C_G:C_G + GATE_RANK] = winf_ref[:, C_G:IN_COLS].astype(BF16)
        wout_ref[...] = woutf_ref[...].astype(BF16)

    @pl.when(t == 0)
    def _():
        smask = _state_mask()
        um = _rms(meta_ref[...], nmw_ref[...]).astype(BF16)
        pm = _dot(um, win_ref[...])
        la = _log_decay(pm[:, C_G:IN_PAD], wgu_ref, bg_ref)
        b = _seg_cumsum(la, N_META)
        k_end = (pm[:, C_K:C_V] * jnp.exp(b[N_META - 1:N_META, :] - b)).astype(BF16)
        vm = pm[:, C_V:C_R].astype(BF16)
        for p in range(N_PAIRS):
            ds = _dot_tn(vm[:, 2 * GLA_HEAD_V * p:2 * GLA_HEAD_V * (p + 1)],
                         k_end[:, LANES * p:LANES * (p + 1)])
            ds = jnp.where(smask, ds, 0.0)
            for s in range(SEQS):
                st_ref[s, p] = ds
        for s in range(SEQS):
            pvh_ref[s, 0:HIST, :] = pm[:, 0:POOL_WIDTH]

    tiles = [
        _mixer_tile(x_ref[s], pvh_ref.at[s], st_ref.at[s], nmw_ref, win_ref, wgu_ref, bg_ref, gnw_ref,
                    pw_ref, ps_ref, wout_ref, nfw_ref, wr_ref, br_ref,
                    hext_ref.at[s, :, 0:D_MODEL], route_ref.at[s], hext_ref.at[s, :, D_MODEL:HEXT])
        for s in range(SEQS)]
    live = list(range(SEQS))
    step = 0
    while live:
        for s in list(live):
            if step >= s and next(tiles[s], "done") == "done":
                live.remove(s)
        step += 1


def _mixer_call(x, meta, nmw, winf, wgu, bg, gnw, pw, ps, woutf, nfw, wr, br):
    bsz, seq, _ = x.shape

    def full(shape, single=False):
        kw = dict(pipeline_mode=pl.Buffered(1)) if single else {}
        return pl.BlockSpec(shape, lambda b, t: (0,) * len(shape), **kw)

    return pl.pallas_call(
        _mixer_kernel,
        grid=(bsz // SEQS, seq // TN),
        in_specs=[
            pl.BlockSpec((SEQS, TN, D_MODEL), lambda b, t: (b, t, 0)),
            full((N_META, D_MODEL)),
            full((1, D_MODEL)),
            full((D_MODEL, IN_COLS), single=True),
            full((LANES, GLA_KDIM)),
            full((1, GLA_KDIM)),
            full((1, GLA_HEAD_V)),
            full((len(POOL_WINDOWS), POOL_GROUP, POOL_GROUP)),
            full((1, POOL_WIDTH)),
            full((D_MODEL, D_MODEL), single=True),
            full((1, D_MODEL)),
            full((2 * ROUTE_ROWS, D_MODEL)),
            full((ROUTE_ROWS, TN)),
        ],
        out_specs=[
            pl.BlockSpec((SEQS, TN, HEXT), lambda b, t: (b, t, 0)),
            pl.BlockSpec((SEQS, 8, TN), lambda b, t: (b, 0, t)),
        ],
        out_shape=[
            jax.ShapeDtypeStruct((bsz, seq, HEXT), F32),
            jax.ShapeDtypeStruct((bsz, 8, seq), F32),
        ],
        scratch_shapes=[
            pltpu.VMEM((D_MODEL, IN_PAD), BF16),
            pltpu.VMEM((D_MODEL, D_MODEL), BF16),
            pltpu.VMEM((SEQS, HIST + TN, POOL_WIDTH), F32),
            pltpu.VMEM((SEQS, N_PAIRS, 2 * GLA_HEAD_V, 2 * GLA_HEAD_K), F32),
        ],
        compiler_params=pltpu.CompilerParams(
            dimension_semantics=("arbitrary", "arbitrary"),
            vmem_limit_bytes=VMEM_LIMIT),
        name="mixer",
    )(x, meta, nmw, winf, wgu, bg, gnw, pw, ps, woutf, nfw, wr, br)


def _row_gather_copy(hext_hbm, hbuf, sem, slot, src_row, dst_row):
    return pltpu.make_async_copy(hext_hbm.at[pl.ds(src_row, 1), :],
                                 hbuf.at[slot, pl.ds(dst_row, 1), :], sem.at[slot])


def _row_scatter_copy(obuf, out_hbm, sem, slot, src_row, dst_row):
    return pltpu.make_async_copy(obuf.at[slot, pl.ds(src_row, 1), :],
                                 out_hbm.at[pl.ds(dst_row, 1), :], sem.at[slot])


def _issue_gathers(idx_ref, hext_hbm, hbuf, gsem, slot):
    def body(rr, carry):
        src = jnp.maximum(idx_ref[0, rr], 0)
        _row_gather_copy(hext_hbm, hbuf, gsem, slot, src, rr).start()
        return carry
    lax.fori_loop(0, TM, body, 0, unroll=8)


def _issue_scatters(idx_ref, obuf, out_hbm, ssem, slot, nrows):
    def body(rr, carry):
        _row_scatter_copy(obuf, out_hbm, ssem, slot, rr, idx_ref[0, rr]).start()
        return carry

    @pl.when(nrows == TM)
    def _():
        lax.fori_loop(0, TM, body, 0, unroll=8)

    @pl.when(nrows < TM)
    def _():
        lax.fori_loop(0, nrows, body, 0)


def _wait_scatters(obuf, out_hbm, ssem, slot, nrows):
    size = TM
    while size >= 1:
        @pl.when((nrows & size) != 0)
        def _(size=size):
            pltpu.make_async_copy(obuf.at[slot, pl.ds(0, size), :],
                                  out_hbm.at[pl.ds(0, size), :], ssem.at[slot]).wait()
        size //= 2


def _moe_kernel(ea_ref, eb_ref, nv_ref, na_ref,
                idx_ref, idxn_ref, hext_hbm, wga_ref, wua_ref, wda_ref, wgb_ref, wub_ref, wdb_ref,
                nfw_ref, fnw_ref,
                out_hbm, hbuf, obuf, wgu_a, wd_a, wgu_b, wd_b, gsem, ssem):
    i = pl.program_id(0)
    n_act = na_ref[0]
    slot = i % 2
    prev = jnp.maximum(i - 1, 0)

    @pl.when(i == 0)
    def _():
        _issue_gathers(idx_ref, hext_hbm, hbuf, gsem, 0)

    @pl.when(i + 1 < n_act)
    def _():
        _issue_gathers(idxn_ref, hext_hbm, hbuf, gsem, 1 - slot)

    @pl.when((i == 0) | (ea_ref[i] != ea_ref[prev]))
    def _():
        wgu_a[:, 0:EXPERT_HIDDEN] = wga_ref[...].astype(BF16)
        wgu_a[:, EXPERT_HIDDEN:2 * EXPERT_HIDDEN] = wua_ref[...].astype(BF16)
        wd_a[...] = wda_ref[...].astype(BF16)

    @pl.when((i == 0) | (eb_ref[i] != eb_ref[prev]))
    def _():
        wgu_b[:, 0:EXPERT_HIDDEN] = wgb_ref[...].astype(BF16)
        wgu_b[:, EXPERT_HIDDEN:2 * EXPERT_HIDDEN] = wub_ref[...].astype(BF16)
        wd_b[...] = wdb_ref[...].astype(BF16)

    @pl.when(i < n_act)
    def _():
        pltpu.make_async_copy(hext_hbm.at[pl.ds(0, TM), :], hbuf.at[slot], gsem.at[slot]).wait()
        rows = hbuf[slot]
        hrow = rows[:, 0:D_MODEL]
        wa = rows[:, D_MODEL + 3:D_MODEL + 4]
        wb = rows[:, D_MODEL + 4:D_MODEL + 5]
        xn = _rms(hrow, nfw_ref[...]).astype(BF16)

        def expert(wgu, wd):
            gu = _dot(xn, wgu[...])
            gate = gu[:, 0:EXPERT_HIDDEN]
            hid = (gate * jax.nn.sigmoid(gate)) * gu[:, EXPERT_HIDDEN:2 * EXPERT_HIDDEN]
            return _dot(hid.astype(BF16), wd[...])

        y = wa * expert(wgu_a, wd_a)
        y = y + wb * expert(wgu_b, wd_b)
        res = _rms(hrow + y, fnw_ref[...])

        @pl.when(i >= 2)
        def _():
            _wait_scatters(obuf, out_hbm, ssem, slot, nv_ref[jnp.maximum(i - 2, 0)])

        obuf[slot] = res
        _issue_scatters(idx_ref, obuf, out_hbm, ssem, slot, nv_ref[i])

    @pl.when(i == n_act - 1)
    def _():
        _wait_scatters(obuf, out_hbm, ssem, slot, nv_ref[i])

        @pl.when(i >= 1)
        def _():
            _wait_scatters(obuf, out_hbm, ssem, 1 - slot, nv_ref[prev])


def _moe_call(ea, eb, nvalid, nact, idx, hext, wg, wu, wd, nfw, fnw, n_tokens):
    n_tiles = idx.shape[0]
    sel_a = lambda i, ea, eb, nv, na: (ea[i], 0, 0)
    sel_b = lambda i, ea, eb, nv, na: (eb[i], 0, 0)
    up_spec = lambda sel: pl.BlockSpec((None, D_MODEL, EXPERT_HIDDEN), sel)
    down_spec = lambda sel: pl.BlockSpec((None, EXPERT_HIDDEN, D_MODEL), sel)
    grid_spec = pltpu.PrefetchScalarGridSpec(
        num_scalar_prefetch=4,
        grid=(n_tiles,),
        in_specs=[
            pl.BlockSpec((None, 1, TM), lambda i, ea, eb, nv, na: (i, 0, 0),
                         memory_space=pltpu.SMEM),
            pl.BlockSpec((None, 1, TM), lambda i, ea, eb, nv, na: (jnp.minimum(i + 1, n_tiles - 1), 0, 0),
                         memory_space=pltpu.SMEM),
            pl.BlockSpec(memory_space=pl.ANY),
            up_spec(sel_a), up_spec(sel_a), down_spec(sel_a),
            up_spec(sel_b), up_spec(sel_b), down_spec(sel_b),
            pl.BlockSpec((1, D_MODEL), lambda i, ea, eb, nv, na: (0, 0)),
            pl.BlockSpec((1, D_MODEL), lambda i, ea, eb, nv, na: (0, 0)),
        ],
        out_specs=pl.BlockSpec(memory_space=pl.ANY),
        scratch_shapes=[
            pltpu.VMEM((2, TM, HEXT), F32),
            pltpu.VMEM((2, TM, D_MODEL), F32),
            pltpu.VMEM((D_MODEL, 2 * EXPERT_HIDDEN), BF16),
            pltpu.VMEM((EXPERT_HIDDEN, D_MODEL), BF16),
            pltpu.VMEM((D_MODEL, 2 * EXPERT_HIDDEN), BF16),
            pltpu.VMEM((EXPERT_HIDDEN, D_MODEL), BF16),
            pltpu.SemaphoreType.DMA((2,)),
            pltpu.SemaphoreType.DMA((2,)),
        ],
    )
    return pl.pallas_call(
        _moe_kernel,
        grid_spec=grid_spec,
        out_shape=jax.ShapeDtypeStruct((n_tokens, D_MODEL), F32),
        compiler_params=pltpu.CompilerParams(
            dimension_semantics=("arbitrary",),
            vmem_limit_bytes=VMEM_LIMIT),
        name="moe",
    )(ea, eb, nvalid, nact, idx, idx, hext, wg, wu, wd, wg, wu, wd, nfw, fnw)


def _slot_tables(route, n_tokens):
    g = route[:, 0, :].reshape(n_tokens).astype(I32)
    ja = route[:, 1, :].reshape(n_tokens).astype(I32)
    jb = route[:, 2, :].reshape(n_tokens).astype(I32)
    pair = (ja * (7 - ja)) // 2 + (jb - ja - 1)
    bucket = g * N_PAIR_BUCKETS + pair
    onehot = (bucket[:, None] == jnp.arange(N_BUCKETS, dtype=I32)[None, :]).astype(I32)
    csum = jnp.cumsum(onehot, axis=0)
    counts = csum[-1]
    rank = jnp.sum(csum * onehot, axis=1) - 1
    padded = ((counts + TM - 1) // TM) * TM
    ends = jnp.cumsum(padded)
    starts = ends - padded
    dst = jnp.take(starts, bucket) + rank
    n_tiles = n_tokens // TM + N_BUCKETS
    src = jnp.full((n_tiles * TM,), -1, I32).at[dst].set(
        jnp.arange(n_tokens, dtype=I32), unique_indices=True)
    tile_start = jnp.arange(n_tiles, dtype=I32) * TM
    total = ends[-1]
    nact = total // TM
    tb = jnp.minimum(jnp.sum((tile_start[:, None] >= ends[None, :]).astype(I32), axis=1), N_BUCKETS - 1)
    valid = tile_start < total
    tb = jnp.where(valid, tb, jnp.take(tb, jnp.maximum(nact - 1, 0)))
    nvalid = jnp.where(valid, jnp.clip(jnp.take(starts, tb) + jnp.take(counts, tb) - tile_start, 0, TM), 0)
    pair_a = jnp.array([0, 0, 0, 1, 1, 2], I32)
    pair_b = jnp.array([1, 2, 3, 2, 3, 3], I32)
    ea = (tb // N_PAIR_BUCKETS) * EXPERTS_PER_GROUP + jnp.take(pair_a, tb % N_PAIR_BUCKETS)
    eb = (tb // N_PAIR_BUCKETS) * EXPERTS_PER_GROUP + jnp.take(pair_b, tb % N_PAIR_BUCKETS)
    return ea, eb, nvalid.astype(I32), jnp.reshape(nact, (1,)).astype(I32), src.reshape(n_tiles, 1, TM)


def kernel(x, meta_tokens, norm_mix_w, w_in, w_gate_up, b_gate, gla_norm_w, pool_w, pool_scale, w_out, norm_ffn_w, w_router_group, b_router_group, w_router_expert, b_router_expert, w_expert_gate, w_expert_up, w_expert_down, final_norm_w):
    bsz, seq, d = x.shape
    assert d == D_MODEL and seq % TN == 0 and bsz % SEQS == 0 and (bsz * seq) % TM == 0
    assert norm_mix_w.shape[0] == 1
    n_tokens = bsz * seq

    wgu_gate = jnp.pad(w_gate_up[0], ((0, LANES - GATE_RANK), (0, 0))).astype(BF16)
    w_r = jnp.pad(jnp.concatenate([w_router_group[0], w_router_expert[0]], axis=1).T,
                  ((0, ROUTE_ROWS - N_GROUPS - N_EXPERTS), (0, 0)))
    w_r_hi = w_r.astype(BF16)
    w_r_lo = (w_r - w_r_hi.astype(F32)).astype(BF16)
    b_r = jnp.pad(jnp.concatenate([b_router_group[0], b_router_expert[0]]),
                  (0, ROUTE_ROWS - N_GROUPS - N_EXPERTS))
    b_r = jnp.broadcast_to(b_r[:, None], (ROUTE_ROWS, TN))

    hext, route = _mixer_call(
        x, meta_tokens, norm_mix_w[0].reshape(1, d), w_in[0], wgu_gate, b_gate[0].reshape(1, GLA_KDIM),
        gla_norm_w[0].reshape(1, GLA_HEAD_V), pool_w[0].astype(BF16), pool_scale[0].reshape(1, POOL_WIDTH),
        w_out[0], norm_ffn_w[0].reshape(1, d), jnp.concatenate([w_r_hi, w_r_lo], axis=0), b_r)

    ea, eb, nvalid, nact, idx = _slot_tables(route, n_tokens)
    out = _moe_call(ea, eb, nvalid, nact, idx, hext.reshape(n_tokens, HEXT),
                    w_expert_gate[0], w_expert_up[0], w_expert_down[0],
                    norm_ffn_w[0].reshape(1, d), final_norm_w.reshape(1, d), n_tokens)
    return out.reshape(bsz, seq, d)
```

```python
import functools

import jax
import jax.numpy as jnp
from jax import lax
from jax.experimental import pallas as pl
from jax.experimental.pallas import tpu as pltpu

F32 = jnp.float32
BF16 = jnp.bfloat16
I32 = jnp.int32

D_MODEL = 1024
N_META = 16
POOL_WIDTH = 512
POOL_WINDOWS = (2, 4, 8, 16)
POOL_GROUP = 128
GLA_HEADS = 4
GLA_VDIM = 512
GLA_HEAD_V = 128
GLA_KDIM = 256
GLA_HEAD_K = 64
GATE_RANK = 16
GATE_NORMALIZER = 16.0
CHUNK = 64
N_GROUPS = 4
EXPERTS_PER_GROUP = 4
N_EXPERTS = 16
EXPERT_HIDDEN = 256
EPS = 1e-6

LANES = 128
SUBLANES = 8
IN_COLS = POOL_WIDTH + 2 * GLA_KDIM + 2 * GLA_VDIM + GATE_RANK
IN_PAD = POOL_WIDTH + 2 * GLA_KDIM + 2 * GLA_VDIM + LANES
C_Q = POOL_WIDTH
C_K = C_Q + GLA_KDIM
C_V = C_K + GLA_KDIM
C_R = C_V + GLA_VDIM
C_G = C_R + GLA_VDIM
HEXT = D_MODEL + LANES
ROUTE_ROWS = 32

TN = 256
SEQS = 2
N_CHUNKS = TN // CHUNK
N_PAIRS = GLA_HEADS // 2
HIST = 16

TM = 256
PB = 2048
N_PAIR_BUCKETS = 6
N_BUCKETS = N_GROUPS * N_PAIR_BUCKETS
VMEM_LIMIT = 56 * 1024 * 1024


def _dot(a, b):
    return jnp.dot(a, b, preferred_element_type=F32)


def _dot_nt(a, b):
    return lax.dot_general(a, b, (((1,), (1,)), ((), ())), preferred_element_type=F32)


def _dot_tn(a, b):
    return lax.dot_general(a, b, (((0,), (0,)), ((), ())), preferred_element_type=F32)


def _rms(x, w):
    return x * lax.rsqrt(jnp.mean(x * x, axis=-1, keepdims=True) + EPS) * w


def _log_sigmoid(z):
    return -(jnp.maximum(-z, 0.0) + jnp.log1p(jnp.exp(-jnp.abs(z))))


def _seg_cumsum(x, seg):
    pos = lax.broadcasted_iota(I32, x.shape, 0) % seg
    s = 1
    while s < seg:
        x = x + jnp.where(pos >= s, pltpu.roll(x, s, axis=0), 0.0)
        s *= 2
    return x


def _log_decay(g, wgu_ref, bg_ref):
    z = _dot(g.astype(BF16), wgu_ref[...]) + bg_ref[...]
    return _log_sigmoid(z) / GATE_NORMALIZER


def _state_mask():
    r = lax.broadcasted_iota(I32, (2 * GLA_HEAD_V, 2 * GLA_HEAD_K), 0) // GLA_HEAD_V
    c = lax.broadcasted_iota(I32, (2 * GLA_HEAD_V, 2 * GLA_HEAD_K), 1) // GLA_HEAD_K
    return r == c


def _first_index(vals, target):
    idx = jnp.full_like(target, float(len(vals) - 1))
    for j in range(len(vals) - 2, -1, -1):
        idx = jnp.where(vals[j] == target, float(j), idx)
    return idx


def _max_list(vals):
    m = vals[0]
    for val in vals[1:]:
        m = jnp.maximum(m, val)
    return m


def _route(lt):
    row = lambda i: lt[i:i + 1, :]
    gl = [row(i) for i in range(N_GROUPS)]
    gmax = _max_list(gl)
    gsel = _first_index(gl, gmax)
    gsum = gl[0] * 0.0
    for val in gl:
        gsum = gsum + jnp.exp(val - gmax)
    p_g = 1.0 / gsum
    el = []
    for j in range(EXPERTS_PER_GROUP):
        val = row(N_GROUPS + EXPERTS_PER_GROUP * (N_GROUPS - 1) + j)
        for g in range(N_GROUPS - 2, -1, -1):
            val = jnp.where(gsel == float(g), row(N_GROUPS + EXPERTS_PER_GROUP * g + j), val)
        el.append(val)
    emax = _max_list(el)
    ee = [jnp.exp(val - emax) for val in el]
    esum = ee[0]
    for val in ee[1:]:
        esum = esum + val
    ep = [val / esum for val in ee]
    p1 = _max_list(ep)
    j1 = _first_index(ep, p1)
    ep_rest = [jnp.where(j1 == float(j), -1.0, ep[j]) for j in range(EXPERTS_PER_GROUP)]
    p2 = _max_list(ep_rest)
    j2 = _first_index(ep_rest, p2)
    den = p1 + p2
    w1 = p_g * (p1 / den)
    w2 = p_g * (p2 / den)
    first_low = j1 < j2
    ja = jnp.where(first_low, j1, j2)
    jb = jnp.where(first_low, j2, j1)
    wa = jnp.where(first_low, w1, w2)
    wb = jnp.where(first_low, w2, w1)
    zero = jnp.zeros_like(wa)
    return jnp.concatenate([gsel, ja, jb, wa, wb, zero, zero, zero], axis=0)


def _mixer_tile(x, pvh_ref, st_ref, nmw_ref, win_ref, wgu_ref, bg_ref, gnw_ref, pw_ref, ps_ref,
                wout_ref, nfw_ref, wr_ref, br_ref, h_out, route_out, wtile_out):
    smask = _state_mask()
    u = _rms(x, nmw_ref[...]).astype(BF16)
    proj = _dot(u, win_ref[...])
    yield

    pv = proj[:, 0:POOL_WIDTH]
    pvh_ref[HIST:HIST + TN, :] = pv
    ext = pvh_ref[...]
    sums = [ext[:, POOL_GROUP * gi:POOL_GROUP * (gi + 1)] for gi in range(len(POOL_WINDOWS))]
    shift = 1
    for stage in range(len(POOL_WINDOWS)):
        for gi in range(stage, len(POOL_WINDOWS)):
            sums[gi] = sums[gi] + pltpu.roll(sums[gi], shift, axis=0)
        shift *= 2
    mixed = []
    for gi, w in enumerate(POOL_WINDOWS):
        cs = slice(POOL_GROUP * gi, POOL_GROUP * (gi + 1))
        pooled = sums[gi][HIST:HIST + TN, :] * (1.0 / w) - pv[:, cs]
        mixed.append(_dot(pooled.astype(BF16), pw_ref[gi]))
    y_pool = jnp.concatenate(mixed, axis=1) * ps_ref[...]
    pvh_ref[0:HIST, :] = ext[TN:TN + HIST, :]

    q = proj[:, C_Q:C_K]
    k = proj[:, C_K:C_V]
    v = proj[:, C_V:C_R].astype(BF16)
    r = proj[:, C_R:C_G]
    la = _log_decay(proj[:, C_G:IN_PAD], wgu_ref, bg_ref)
    b = _seg_cumsum(la, CHUNK)
    b3 = b.reshape(N_CHUNKS, CHUNK, GLA_KDIM)
    bl = b3[:, CHUNK - 1:CHUNK, :]
    blb = jnp.broadcast_to(bl, (N_CHUNKS, CHUNK, GLA_KDIM)).reshape(TN, GLA_KDIM)
    q_t = ((q * (GLA_HEAD_K ** -0.5)) * jnp.exp(b)).astype(BF16)
    k_t = (k * jnp.exp(-b)).astype(BF16)
    k_end = (k * jnp.exp(blb - b)).astype(BF16)
    dec = jnp.exp(bl)
    yield

    ri = lax.broadcasted_iota(I32, (TN, TN), 0)
    ci = lax.broadcasted_iota(I32, (TN, TN), 1)
    causal = (ri // CHUNK == ci // CHUNK) & (ci <= ri)
    lane_k = lax.broadcasted_iota(I32, (TN, LANES), 1) // GLA_HEAD_K

    o_heads = [None] * GLA_HEADS
    for p in range(N_PAIRS):
        ks = slice(LANES * p, LANES * (p + 1))
        vs = slice(2 * GLA_HEAD_V * p, 2 * GLA_HEAD_V * (p + 1))
        qp = q_t[:, ks]
        kp = k_t[:, ks]
        o_inter = []
        st = st_ref[p]
        for c in range(N_CHUNKS):
            rs = slice(CHUNK * c, CHUNK * (c + 1))
            o_inter.append(_dot_nt(qp[rs], st.astype(BF16)))
            ds = _dot_tn(v[rs, vs], k_end[rs, ks])
            st = st * dec[c][:, ks] + jnp.where(smask, ds, 0.0)
        st_ref[p] = st
        o_inter = jnp.concatenate(o_inter, axis=0)
        for hh in range(2):
            head = 2 * p + hh
            qm = jnp.where(lane_k == hh, qp, jnp.zeros_like(qp))
            sc = jnp.where(causal, _dot_nt(qm, kp), 0.0).astype(BF16)
            hv = slice(GLA_HEAD_V * head, GLA_HEAD_V * (head + 1))
            o_heads[head] = _dot(sc, v[:, hv]) + o_inter[:, GLA_HEAD_V * hh:GLA_HEAD_V * (hh + 1)]

    gnw = gnw_ref[...]
    y_heads = []
    for head in range(GLA_HEADS):
        o = o_heads[head]
        y_heads.append(o * lax.rsqrt(jnp.mean(o * o, axis=-1, keepdims=True) + EPS) * gnw)
    y_gla = jnp.concatenate(y_heads, axis=1) * (r * jax.nn.sigmoid(r))
    yield

    ycat = jnp.concatenate([y_pool, y_gla], axis=1).astype(BF16)
    h = x + _dot(ycat, wout_ref[...])
    h_out[...] = h
    yield

    xn = _rms(h, nfw_ref[...])
    xh = xn.astype(BF16)
    xl = (xn - xh.astype(F32)).astype(BF16)
    ph = _dot_nt(wr_ref[...], xh)
    plo = _dot_nt(wr_ref[...], xl)
    lt = ((ph[0:ROUTE_ROWS] + ph[ROUTE_ROWS:2 * ROUTE_ROWS])
          + (plo[0:ROUTE_ROWS] + plo[ROUTE_ROWS:2 * ROUTE_ROWS])) + br_ref[...]
    route = _route(lt)
    route_out[...] = route
    wtile_out[...] = jnp.concatenate([route, jnp.zeros((LANES - 8, TN), F32)], axis=0).T


def _mixer_kernel(x_ref, meta_ref, nmw_ref, winf_ref, wgu_ref, bg_ref, gnw_ref, pw_ref, ps_ref,
                  woutf_ref, nfw_ref, wr_ref, br_ref,
                  hext_ref, route_ref, win_ref, wout_ref, pvh_ref, st_ref):
    bi = pl.program_id(0)
    t = pl.program_id(1)

    @pl.when((bi == 0) & (t == 0))
    def _():
        win_ref[:, 0:C_G] = winf_ref[:, 0:C_G].astype(BF16)
        win_ref[:, C_G:IN_PAD] = jnp.zeros((D_MODEL, LANES), BF16)
        win_ref[:, C_G:C_G + GATE_RANK] = winf_ref[:, C_G:IN_COLS].astype(BF16)
        wout_ref[...] = woutf_ref[...].astype(BF16)

    @pl.when(t == 0)
    def _():
        smask = _state_mask()
        um = _rms(meta_ref[...], nmw_ref[...]).astype(BF16)
        pm = _dot(um, win_ref[...])
        la = _log_decay(pm[:, C_G:IN_PAD], wgu_ref, bg_ref)
        b = _seg_cumsum(la, N_META)
        k_end = (pm[:, C_K:C_V] * jnp.exp(b[N_META - 1:N_META, :] - b)).astype(BF16)
        vm = pm[:, C_V:C_R].astype(BF16)
        for p in range(N_PAIRS):
            ds = _dot_tn(vm[:, 2 * GLA_HEAD_V * p:2 * GLA_HEAD_V * (p + 1)],
                         k_end[:, LANES * p:LANES * (p + 1)])
            ds = jnp.where(smask, ds, 0.0)
            for s in range(SEQS):
                st_ref[s, p] = ds
        for s in range(SEQS):
            pvh_ref[s, 0:HIST, :] = pm[:, 0:POOL_WIDTH]

    tiles = [
        _mixer_tile(x_ref[s], pvh_ref.at[s], st_ref.at[s], nmw_ref, win_ref, wgu_ref, bg_ref, gnw_ref,
                    pw_ref, ps_ref, wout_ref, nfw_ref, wr_ref, br_ref,
                    hext_ref.at[s, :, 0:D_MODEL], route_ref.at[s], hext_ref.at[s, :, D_MODEL:HEXT])
        for s in range(SEQS)]
    live = list(range(SEQS))
    step = 0
    while live:
        for s in list(live):
            if step >= s and next(tiles[s], "done") == "done":
                live.remove(s)
        step += 1


def _mixer_call(x, meta, nmw, winf, wgu, bg, gnw, pw, ps, woutf, nfw, wr, br):
    bsz, seq, _ = x.shape

    def full(shape, single=False):
        kw = dict(pipeline_mode=pl.Buffered(1)) if single else {}
        return pl.BlockSpec(shape, lambda b, t: (0,) * len(shape), **kw)

    return pl.pallas_call(
        _mixer_kernel,
        grid=(bsz // SEQS, seq // TN),
        in_specs=[
            pl.BlockSpec((SEQS, TN, D_MODEL), lambda b, t: (b, t, 0)),
            full((N_META, D_MODEL)),
            full((1, D_MODEL)),
            full((D_MODEL, IN_COLS), single=True),
            full((LANES, GLA_KDIM)),
            full((1, GLA_KDIM)),
            full((1, GLA_HEAD_V)),
            full((len(POOL_WINDOWS), POOL_GROUP, POOL_GROUP)),
            full((1, POOL_WIDTH)),
            full((D_MODEL, D_MODEL), single=True),
            full((1, D_MODEL)),
            full((2 * ROUTE_ROWS, D_MODEL)),
            full((ROUTE_ROWS, TN)),
        ],
        out_specs=[
            pl.BlockSpec((SEQS, TN, HEXT), lambda b, t: (b, t, 0)),
            pl.BlockSpec((SEQS, 8, TN), lambda b, t: (b, 0, t)),
        ],
        out_shape=[
            jax.ShapeDtypeStruct((bsz, seq, HEXT), F32),
            jax.ShapeDtypeStruct((bsz, 8, seq), F32),
        ],
        scratch_shapes=[
            pltpu.VMEM((D_MODEL, IN_PAD), BF16),
            pltpu.VMEM((D_MODEL, D_MODEL), BF16),
            pltpu.VMEM((SEQS, HIST + TN, POOL_WIDTH), F32),
            pltpu.VMEM((SEQS, N_PAIRS, 2 * GLA_HEAD_V, 2 * GLA_HEAD_K), F32),
        ],
        compiler_params=pltpu.CompilerParams(
            dimension_semantics=("arbitrary", "arbitrary"),
            vmem_limit_bytes=VMEM_LIMIT),
        name="mixer",
    )(x, meta, nmw, winf, wgu, bg, gnw, pw, ps, woutf, nfw, wr, br)


def _permute_rows_kernel(zoff_ref, zflag_ref, slot_ref, src_hbm, dst_hbm, zeros_ref, sem, zsem, *,
                         to_slots, zero_fill):
    i = pl.program_id(0)
    n_steps = pl.num_programs(0)

    if zero_fill:
        @pl.when(i == 0)
        def _():
            zeros_ref[...] = jnp.zeros(zeros_ref.shape, zeros_ref.dtype)
            for b in range(zflag_ref.shape[0]):
                @pl.when(zflag_ref[b] != 0)
                def _(b=b):
                    off = pl.multiple_of(zoff_ref[b], TM)
                    pltpu.make_async_copy(zeros_ref, dst_hbm.at[pl.ds(off, TM), :], zsem).start()
            for b in range(zflag_ref.shape[0]):
                @pl.when(zflag_ref[b] != 0)
                def _(b=b):
                    off = pl.multiple_of(zoff_ref[b], TM)
                    pltpu.make_async_copy(zeros_ref, dst_hbm.at[pl.ds(off, TM), :], zsem).wait()

    base = i * PB

    def body(rr, carry):
        tok = base + rr
        slot = slot_ref[0, rr]
        src_row, dst_row = (tok, slot) if to_slots else (slot, tok)
        pltpu.make_async_copy(src_hbm.at[pl.ds(src_row, 1), :],
                              dst_hbm.at[pl.ds(dst_row, 1), :], sem).start()
        return carry
    lax.fori_loop(0, PB, body, 0, unroll=8)

    def wait_block():
        pltpu.make_async_copy(src_hbm.at[pl.ds(0, PB), :], dst_hbm.at[pl.ds(0, PB), :], sem).wait()

    @pl.when(i >= 1)
    def _():
        wait_block()

    @pl.when(i == n_steps - 1)
    def _():
        wait_block()


def _permute_rows_call(zoff, zflag, slots, src, n_out_rows, to_slots, zero_fill, name):
    n_steps = slots.shape[0]
    width = src.shape[1]
    grid_spec = pltpu.PrefetchScalarGridSpec(
        num_scalar_prefetch=2,
        grid=(n_steps,),
        in_specs=[
            pl.BlockSpec((None, 1, PB), lambda i, zo, zf: (i, 0, 0), memory_space=pltpu.SMEM),
            pl.BlockSpec(memory_space=pl.ANY),
        ],
        out_specs=pl.BlockSpec(memory_space=pl.ANY),
        scratch_shapes=[
            pltpu.VMEM((TM, width), F32),
            pltpu.SemaphoreType.DMA(()),
            pltpu.SemaphoreType.DMA(()),
        ],
    )
    return pl.pallas_call(
        functools.partial(_permute_rows_kernel, to_slots=to_slots, zero_fill=zero_fill),
        grid_spec=grid_spec,
        out_shape=jax.ShapeDtypeStruct((n_out_rows, width), F32),
        compiler_params=pltpu.CompilerParams(dimension_semantics=("arbitrary",)),
        name=name,
    )(zoff, zflag, slots, src)


def _moe_kernel(ea_ref, eb_ref, na_ref,
                rows_ref, wga_ref, wua_ref, wda_ref, wgb_ref, wub_ref, wdb_ref,
                nfw_ref, fnw_ref,
                out_ref, wgu_a, wd_a, wgu_b, wd_b):
    i = pl.program_id(0)
    n_act = na_ref[0]
    prev = jnp.maximum(i - 1, 0)

    @pl.when((i == 0) | (ea_ref[i] != ea_ref[prev]))
    def _():
        wgu_a[:, 0:EXPERT_HIDDEN] = wga_ref[...].astype(BF16)
        wgu_a[:, EXPERT_HIDDEN:2 * EXPERT_HIDDEN] = wua_ref[...].astype(BF16)
        wd_a[...] = wda_ref[...].astype(BF16)

    @pl.when((i == 0) | (eb_ref[i] != eb_ref[prev]))
    def _():
        wgu_b[:, 0:EXPERT_HIDDEN] = wgb_ref[...].astype(BF16)
        wgu_b[:, EXPERT_HIDDEN:2 * EXPERT_HIDDEN] = wub_ref[...].astype(BF16)
        wd_b[...] = wdb_ref[...].astype(BF16)

    @pl.when(i < n_act)
    def _():
        rows = rows_ref[...]
        hrow = rows[:, 0:D_MODEL]
        wa = rows[:, D_MODEL + 3:D_MODEL + 4]
        wb = rows[:, D_MODEL + 4:D_MODEL + 5]
        xn = _rms(hrow, nfw_ref[...]).astype(BF16)

        def expert(wgu, wd):
            gu = _dot(xn, wgu[...])
            gate = gu[:, 0:EXPERT_HIDDEN]
            hid = (gate * jax.nn.sigmoid(gate)) * gu[:, EXPERT_HIDDEN:2 * EXPERT_HIDDEN]
            return _dot(hid.astype(BF16), wd[...])

        y = wa * expert(wgu_a, wd_a)
        y = y + wb * expert(wgu_b, wd_b)
        out_ref[...] = _rms(hrow + y, fnw_ref[...])

    @pl.when(i >= n_act)
    def _():
        out_ref[...] = jnp.zeros(out_ref.shape, out_ref.dtype)


def _moe_call(ea, eb, nact, rows, wg, wu, wd, nfw, fnw):
    n_tiles = rows.shape[0] // TM
    sel_a = lambda i, ea, eb, na: (ea[i], 0, 0)
    sel_b = lambda i, ea, eb, na: (eb[i], 0, 0)
    tile = lambda i, ea, eb, na: (jnp.minimum(i, na[0] - 1), 0)
    up_spec = lambda sel: pl.BlockSpec((None, D_MODEL, EXPERT_HIDDEN), sel)
    down_spec = lambda sel: pl.BlockSpec((None, EXPERT_HIDDEN, D_MODEL), sel)
    grid_spec = pltpu.PrefetchScalarGridSpec(
        num_scalar_prefetch=3,
        grid=(n_tiles,),
        in_specs=[
            pl.BlockSpec((TM, HEXT), tile),
            up_spec(sel_a), up_spec(sel_a), down_spec(sel_a),
            up_spec(sel_b), up_spec(sel_b), down_spec(sel_b),
            pl.BlockSpec((1, D_MODEL), lambda i, ea, eb, na: (0, 0)),
            pl.BlockSpec((1, D_MODEL), lambda i, ea, eb, na: (0, 0)),
        ],
        out_specs=pl.BlockSpec((TM, D_MODEL), lambda i, ea, eb, na: (i, 0)),
        scratch_shapes=[
            pltpu.VMEM((D_MODEL, 2 * EXPERT_HIDDEN), BF16),
            pltpu.VMEM((EXPERT_HIDDEN, D_MODEL), BF16),
            pltpu.VMEM((D_MODEL, 2 * EXPERT_HIDDEN), BF16),
            pltpu.VMEM((EXPERT_HIDDEN, D_MODEL), BF16),
        ],
    )
    return pl.pallas_call(
        _moe_kernel,
        grid_spec=grid_spec,
        out_shape=jax.ShapeDtypeStruct((n_tiles * TM, D_MODEL), F32),
        compiler_params=pltpu.CompilerParams(
            dimension_semantics=("arbitrary",),
            vmem_limit_bytes=VMEM_LIMIT),
        name="moe",
    )(ea, eb, nact, rows, wg, wu, wd, wg, wu, wd, nfw, fnw)


def _slot_tables(route, n_tokens):
    g = route[:, 0, :].reshape(n_tokens).astype(I32)
    ja = route[:, 1, :].reshape(n_tokens).astype(I32)
    jb = route[:, 2, :].reshape(n_tokens).astype(I32)
    pair = (ja * (7 - ja)) // 2 + (jb - ja - 1)
    bucket = g * N_PAIR_BUCKETS + pair
    onehot = (bucket[:, None] == jnp.arange(N_BUCKETS, dtype=I32)[None, :]).astype(I32)
    csum = jnp.cumsum(onehot, axis=0)
    counts = csum[-1]
    rank = jnp.sum(csum * onehot, axis=1) - 1
    padded = ((counts + TM - 1) // TM) * TM
    ends = jnp.cumsum(padded)
    starts = ends - padded
    dst = jnp.take(starts, bucket) + rank
    n_tiles = n_tokens // TM + N_BUCKETS
    tile_start = jnp.arange(n_tiles, dtype=I32) * TM
    total = ends[-1]
    nact = total // TM
    tb = jnp.minimum(jnp.sum((tile_start[:, None] >= ends[None, :]).astype(I32), axis=1), N_BUCKETS - 1)
    tb = jnp.where(tile_start < total, tb, jnp.take(tb, jnp.maximum(nact - 1, 0)))
    pair_a = jnp.array([0, 0, 0, 1, 1, 2], I32)
    pair_b = jnp.array([1, 2, 3, 2, 3, 3], I32)
    ea = (tb // N_PAIR_BUCKETS) * EXPERTS_PER_GROUP + jnp.take(pair_a, tb % N_PAIR_BUCKETS)
    eb = (tb // N_PAIR_BUCKETS) * EXPERTS_PER_GROUP + jnp.take(pair_b, tb % N_PAIR_BUCKETS)
    tail = nact + jnp.arange(N_BUCKETS, dtype=I32)
    zflag = jnp.concatenate([(counts % TM != 0).astype(I32), (tail < n_tiles).astype(I32)])
    zoff = jnp.concatenate([jnp.maximum(ends - TM, 0), jnp.minimum(tail, n_tiles - 1) * TM])
    return (ea, eb, jnp.reshape(nact, (1,)).astype(I32), zoff.astype(I32), zflag,
            dst.reshape(n_tokens // PB, 1, PB))


def kernel(x, meta_tokens, norm_mix_w, w_in, w_gate_up, b_gate, gla_norm_w, pool_w, pool_scale, w_out, norm_ffn_w, w_router_group, b_router_group, w_router_expert, b_router_expert, w_expert_gate, w_expert_up, w_expert_down, final_norm_w):
    bsz, seq, d = x.shape
    assert d == D_MODEL and seq % TN == 0 and bsz % SEQS == 0 and (bsz * seq) % TM == 0
    assert norm_mix_w.shape[0] == 1
    n_tokens = bsz * seq

    wgu_gate = jnp.pad(w_gate_up[0], ((0, LANES - GATE_RANK), (0, 0))).astype(BF16)
    w_r = jnp.pad(jnp.concatenate([w_router_group[0], w_router_expert[0]], axis=1).T,
                  ((0, ROUTE_ROWS - N_GROUPS - N_EXPERTS), (0, 0)))
    w_r_hi = w_r.astype(BF16)
    w_r_lo = (w_r - w_r_hi.astype(F32)).astype(BF16)
    b_r = jnp.pad(jnp.concatenate([b_router_group[0], b_router_expert[0]]),
                  (0, ROUTE_ROWS - N_GROUPS - N_EXPERTS))
    b_r = jnp.broadcast_to(b_r[:, None], (ROUTE_ROWS, TN))

    hext, route = _mixer_call(
        x, meta_tokens, norm_mix_w[0].reshape(1, d), w_in[0], wgu_gate, b_gate[0].reshape(1, GLA_KDIM),
        gla_norm_w[0].reshape(1, GLA_HEAD_V), pool_w[0].astype(BF16), pool_scale[0].reshape(1, POOL_WIDTH),
        w_out[0], norm_ffn_w[0].reshape(1, d), jnp.concatenate([w_r_hi, w_r_lo], axis=0), b_r)

    ea, eb, nact, zoff, zflag, slots = _slot_tables(route, n_tokens)
    n_slots = n_tokens + N_BUCKETS * TM
    rows = _permute_rows_call(zoff, zflag, slots, hext.reshape(n_tokens, HEXT), n_slots,
                              to_slots=True, zero_fill=True, name="dispatch")
    res = _moe_call(ea, eb, nact, rows, w_expert_gate[0], w_expert_up[0], w_expert_down[0],
                    norm_ffn_w[0].reshape(1, d), final_norm_w.reshape(1, d))
    out = _permute_rows_call(zoff, zflag, slots, res, n_tokens,
                             to_slots=False, zero_fill=False, name="combine")
    return out.reshape(bsz, seq, d)
```

```python
import jax
import jax.numpy as jnp
from jax import lax
from jax.experimental import pallas as pl
from jax.experimental.pallas import tpu as pltpu

F32 = jnp.float32
BF16 = jnp.bfloat16
I32 = jnp.int32

D_MODEL = 1024
N_META = 16
POOL_WIDTH = 512
POOL_WINDOWS = (2, 4, 8, 16)
POOL_GROUP = 128
GLA_HEADS = 4
GLA_VDIM = 512
GLA_HEAD_V = 128
GLA_KDIM = 256
GLA_HEAD_K = 64
GATE_RANK = 16
GATE_NORMALIZER = 16.0
CHUNK = 64
N_GROUPS = 4
EXPERTS_PER_GROUP = 4
N_EXPERTS = 16
EXPERT_HIDDEN = 256
EPS = 1e-6

LANES = 128
SUBLANES = 8
IN_COLS = POOL_WIDTH + 2 * GLA_KDIM + 2 * GLA_VDIM + GATE_RANK
IN_PAD = POOL_WIDTH + 2 * GLA_KDIM + 2 * GLA_VDIM + LANES
C_Q = POOL_WIDTH
C_K = C_Q + GLA_KDIM
C_V = C_K + GLA_KDIM
C_R = C_V + GLA_VDIM
C_G = C_R + GLA_VDIM
HEXT = D_MODEL + LANES
ROUTE_ROWS = 32

TN = 256
SEQS = 2
N_CHUNKS = TN // CHUNK
N_PAIRS = GLA_HEADS // 2
HIST = 16

TM = 256
PB = 2048
N_PAIR_BUCKETS = 6
N_BUCKETS = N_GROUPS * N_PAIR_BUCKETS
VMEM_LIMIT = 56 * 1024 * 1024


def _dot(a, b):
    return jnp.dot(a, b, preferred_element_type=F32)


def _dot_nt(a, b):
    return lax.dot_general(a, b, (((1,), (1,)), ((), ())), preferred_element_type=F32)


def _dot_tn(a, b):
    return lax.dot_general(a, b, (((0,), (0,)), ((), ())), preferred_element_type=F32)


def _rms(x, w):
    return x * lax.rsqrt(jnp.mean(x * x, axis=-1, keepdims=True) + EPS) * w


def _log_sigmoid(z):
    return -(jnp.maximum(-z, 0.0) + jnp.log1p(jnp.exp(-jnp.abs(z))))


def _seg_cumsum(x, seg):
    pos = lax.broadcasted_iota(I32, x.shape, 0) % seg
    s = 1
    while s < seg:
        x = x + jnp.where(pos >= s, pltpu.roll(x, s, axis=0), 0.0)
        s *= 2
    return x


def _log_decay(g, wgu_ref, bg_ref):
    z = _dot(g.astype(BF16), wgu_ref[...]) + bg_ref[...]
    return _log_sigmoid(z) / GATE_NORMALIZER


def _state_mask():
    r = lax.broadcasted_iota(I32, (2 * GLA_HEAD_V, 2 * GLA_HEAD_K), 0) // GLA_HEAD_V
    c = lax.broadcasted_iota(I32, (2 * GLA_HEAD_V, 2 * GLA_HEAD_K), 1) // GLA_HEAD_K
    return r == c


def _first_index(vals, target):
    idx = jnp.full_like(target, float(len(vals) - 1))
    for j in range(len(vals) - 2, -1, -1):
        idx = jnp.where(vals[j] == target, float(j), idx)
    return idx


def _max_list(vals):
    m = vals[0]
    for val in vals[1:]:
        m = jnp.maximum(m, val)
    return m


def _route(lt):
    row = lambda i: lt[i:i + 1, :]
    gl = [row(i) for i in range(N_GROUPS)]
    gmax = _max_list(gl)
    gsel = _first_index(gl, gmax)
    gsum = gl[0] * 0.0
    for val in gl:
        gsum = gsum + jnp.exp(val - gmax)
    p_g = 1.0 / gsum
    el = []
    for j in range(EXPERTS_PER_GROUP):
        val = row(N_GROUPS + EXPERTS_PER_GROUP * (N_GROUPS - 1) + j)
        for g in range(N_GROUPS - 2, -1, -1):
            val = jnp.where(gsel == float(g), row(N_GROUPS + EXPERTS_PER_GROUP * g + j), val)
        el.append(val)
    emax = _max_list(el)
    ee = [jnp.exp(val - emax) for val in el]
    esum = ee[0]
    for val in ee[1:]:
        esum = esum + val
    ep = [val / esum for val in ee]
    p1 = _max_list(ep)
    j1 = _first_index(ep, p1)
    ep_rest = [jnp.where(j1 == float(j), -1.0, ep[j]) for j in range(EXPERTS_PER_GROUP)]
    p2 = _max_list(ep_rest)
    j2 = _first_index(ep_rest, p2)
    den = p1 + p2
    w1 = p_g * (p1 / den)
    w2 = p_g * (p2 / den)
    first_low = j1 < j2
    ja = jnp.where(first_low, j1, j2)
    jb = jnp.where(first_low, j2, j1)
    wa = jnp.where(first_low, w1, w2)
    wb = jnp.where(first_low, w2, w1)
    zero = jnp.zeros_like(wa)
    return jnp.concatenate([gsel, ja, jb, wa, wb, zero, zero, zero], axis=0)


def _mixer_tile(x, pvh_ref, st_ref, nmw_ref, win_ref, wgu_ref, bg_ref, gnw_ref, pw_ref, ps_ref,
                wout_ref, nfw_ref, wr_ref, br_ref, h_out, route_out, wtile_out):
    smask = _state_mask()
    u = _rms(x, nmw_ref[...]).astype(BF16)
    proj = _dot(u, win_ref[...])
    yield

    pv = proj[:, 0:POOL_WIDTH]
    pvh_ref[HIST:HIST + TN, :] = pv
    ext = pvh_ref[...]
    sums = [ext[:, POOL_GROUP * gi:POOL_GROUP * (gi + 1)] for gi in range(len(POOL_WINDOWS))]
    shift = 1
    for stage in range(len(POOL_WINDOWS)):
        for gi in range(stage, len(POOL_WINDOWS)):
            sums[gi] = sums[gi] + pltpu.roll(sums[gi], shift, axis=0)
        shift *= 2
    mixed = []
    for gi, w in enumerate(POOL_WINDOWS):
        cs = slice(POOL_GROUP * gi, POOL_GROUP * (gi + 1))
        pooled = sums[gi][HIST:HIST + TN, :] * (1.0 / w) - pv[:, cs]
        mixed.append(_dot(pooled.astype(BF16), pw_ref[gi]))
    y_pool = jnp.concatenate(mixed, axis=1) * ps_ref[...]
    pvh_ref[0:HIST, :] = ext[TN:TN + HIST, :]

    q = proj[:, C_Q:C_K]
    k = proj[:, C_K:C_V]
    v = proj[:, C_V:C_R].astype(BF16)
    r = proj[:, C_R:C_G]
    la = _log_decay(proj[:, C_G:IN_PAD], wgu_ref, bg_ref)
    b = _seg_cumsum(la, CHUNK)
    b3 = b.reshape(N_CHUNKS, CHUNK, GLA_KDIM)
    bl = b3[:, CHUNK - 1:CHUNK, :]
    blb = jnp.broadcast_to(bl, (N_CHUNKS, CHUNK, GLA_KDIM)).reshape(TN, GLA_KDIM)
    q_t = ((q * (GLA_HEAD_K ** -0.5)) * jnp.exp(b)).astype(BF16)
    k_t = (k * jnp.exp(-b)).astype(BF16)
    k_end = (k * jnp.exp(blb - b)).astype(BF16)
    dec = jnp.exp(bl)
    yield

    ri = lax.broadcasted_iota(I32, (TN, TN), 0)
    ci = lax.broadcasted_iota(I32, (TN, TN), 1)
    causal = (ri // CHUNK == ci // CHUNK) & (ci <= ri)
    lane_k = lax.broadcasted_iota(I32, (TN, LANES), 1) // GLA_HEAD_K

    o_heads = [None] * GLA_HEADS
    for p in range(N_PAIRS):
        ks = slice(LANES * p, LANES * (p + 1))
        vs = slice(2 * GLA_HEAD_V * p, 2 * GLA_HEAD_V * (p + 1))
        qp = q_t[:, ks]
        kp = k_t[:, ks]
        o_inter = []
        st = st_ref[p]
        for c in range(N_CHUNKS):
            rs = slice(CHUNK * c, CHUNK * (c + 1))
            o_inter.append(_dot_nt(qp[rs], st.astype(BF16)))
            ds = _dot_tn(v[rs, vs], k_end[rs, ks])
            st = st * dec[c][:, ks] + jnp.where(smask, ds, 0.0)
        st_ref[p] = st
        o_inter = jnp.concatenate(o_inter, axis=0)
        for hh in range(2):
            head = 2 * p + hh
            qm = jnp.where(lane_k == hh, qp, jnp.zeros_like(qp))
            sc = jnp.where(causal, _dot_nt(qm, kp), 0.0).astype(BF16)
            hv = slice(GLA_HEAD_V * head, GLA_HEAD_V * (head + 1))
            o_heads[head] = _dot(sc, v[:, hv]) + o_inter[:, GLA_HEAD_V * hh:GLA_HEAD_V * (hh + 1)]

    gnw = gnw_ref[...]
    y_heads = []
    for head in range(GLA_HEADS):
        o = o_heads[head]
        y_heads.append(o * lax.rsqrt(jnp.mean(o * o, axis=-1, keepdims=True) + EPS) * gnw)
    y_gla = jnp.concatenate(y_heads, axis=1) * (r * jax.nn.sigmoid(r))
    yield

    ycat = jnp.concatenate([y_pool, y_gla], axis=1).astype(BF16)
    h = x + _dot(ycat, wout_ref[...])
    h_out[...] = h
    yield

    xn = _rms(h, nfw_ref[...])
    xh = xn.astype(BF16)
    xl = (xn - xh.astype(F32)).astype(BF16)
    ph = _dot_nt(wr_ref[...], xh)
    plo = _dot_nt(wr_ref[...], xl)
    lt = ((ph[0:ROUTE_ROWS] + ph[ROUTE_ROWS:2 * ROUTE_ROWS])
          + (plo[0:ROUTE_ROWS] + plo[ROUTE_ROWS:2 * ROUTE_ROWS])) + br_ref[...]
    route = _route(lt)
    route_out[...] = route
    wtile_out[...] = jnp.concatenate([route, jnp.zeros((LANES - 8, TN), F32)], axis=0).T


def _mixer_kernel(x_ref, meta_ref, nmw_ref, winf_ref, wgu_ref, bg_ref, gnw_ref, pw_ref, ps_ref,
                  woutf_ref, nfw_ref, wr_ref, br_ref,
                  hext_ref, route_ref, win_ref, wout_ref, pvh_ref, st_ref):
    bi = pl.program_id(0)
    t = pl.program_id(1)

    @pl.when((bi == 0) & (t == 0))
    def _():
        win_ref[:, 0:C_G] = winf_ref[:, 0:C_G].astype(BF16)
        win_ref[:, C_G:IN_PAD] = jnp.zeros((D_MODEL, LANES), BF16)
        win_ref[:, C_G:C_G + GATE_RANK] = winf_ref[:, C_G:IN_COLS].astype(BF16)
        wout_ref[...] = woutf_ref[...].astype(BF16)

    @pl.when(t == 0)
    def _():
        smask = _state_mask()
        um = _rms(meta_ref[...], nmw_ref[...]).astype(BF16)
        pm = _dot(um, win_ref[...])
        la = _log_decay(pm[:, C_G:IN_PAD], wgu_ref, bg_ref)
        b = _seg_cumsum(la, N_META)
        k_end = (pm[:, C_K:C_V] * jnp.exp(b[N_META - 1:N_META, :] - b)).astype(BF16)
        vm = pm[:, C_V:C_R].astype(BF16)
        for p in range(N_PAIRS):
            ds = _dot_tn(vm[:, 2 * GLA_HEAD_V * p:2 * GLA_HEAD_V * (p + 1)],
                         k_end[:, LANES * p:LANES * (p + 1)])
            ds = jnp.where(smask, ds, 0.0)
            for s in range(SEQS):
                st_ref[s, p] = ds
        for s in range(SEQS):
            pvh_ref[s, 0:HIST, :] = pm[:, 0:POOL_WIDTH]

    tiles = [
        _mixer_tile(x_ref[s], pvh_ref.at[s], st_ref.at[s], nmw_ref, win_ref, wgu_ref, bg_ref, gnw_ref,
                    pw_ref, ps_ref, wout_ref, nfw_ref, wr_ref, br_ref,
                    hext_ref.at[s, :, 0:D_MODEL], route_ref.at[s], hext_ref.at[s, :, D_MODEL:HEXT])
        for s in range(SEQS)]
    live = list(range(SEQS))
    step = 0
    while live:
        for s in list(live):
            if step >= s and next(tiles[s], "done") == "done":
                live.remove(s)
        step += 1


def _mixer_call(x, meta, nmw, winf, wgu, bg, gnw, pw, ps, woutf, nfw, wr, br):
    bsz, seq, _ = x.shape

    def full(shape, single=False):
        kw = dict(pipeline_mode=pl.Buffered(1)) if single else {}
        return pl.BlockSpec(shape, lambda b, t: (0,) * len(shape), **kw)

    return pl.pallas_call(
        _mixer_kernel,
        grid=(bsz // SEQS, seq // TN),
        in_specs=[
            pl.BlockSpec((SEQS, TN, D_MODEL), lambda b, t: (b, t, 0)),
            full((N_META, D_MODEL)),
            full((1, D_MODEL)),
            full((D_MODEL, IN_COLS), single=True),
            full((LANES, GLA_KDIM)),
            full((1, GLA_KDIM)),
            full((1, GLA_HEAD_V)),
            full((len(POOL_WINDOWS), POOL_GROUP, POOL_GROUP)),
            full((1, POOL_WIDTH)),
            full((D_MODEL, D_MODEL), single=True),
            full((1, D_MODEL)),
            full((2 * ROUTE_ROWS, D_MODEL)),
            full((ROUTE_ROWS, TN)),
        ],
        out_specs=[
            pl.BlockSpec((SEQS, TN, HEXT), lambda b, t: (b, t, 0)),
            pl.BlockSpec((SEQS, 8, TN), lambda b, t: (b, 0, t)),
        ],
        out_shape=[
            jax.ShapeDtypeStruct((bsz, seq, HEXT), F32),
            jax.ShapeDtypeStruct((bsz, 8, seq), F32),
        ],
        scratch_shapes=[
            pltpu.VMEM((D_MODEL, IN_PAD), BF16),
            pltpu.VMEM((D_MODEL, D_MODEL), BF16),
            pltpu.VMEM((SEQS, HIST + TN, POOL_WIDTH), F32),
            pltpu.VMEM((SEQS, N_PAIRS, 2 * GLA_HEAD_V, 2 * GLA_HEAD_K), F32),
        ],
        compiler_params=pltpu.CompilerParams(
            dimension_semantics=("arbitrary", "arbitrary"),
            vmem_limit_bytes=VMEM_LIMIT),
        name="mixer",
    )(x, meta, nmw, winf, wgu, bg, gnw, pw, ps, woutf, nfw, wr, br)


def _issue_row_copies(slot_ref, tok_ref, slot_hbm, sem, to_slots):
    def body(gi, carry):
        for j in range(SUBLANES):
            slot = slot_ref[0, gi * SUBLANES + j]
            tok_row = tok_ref.at[gi, pl.ds(j, 1), :]
            slot_row = slot_hbm.at[pl.ds(slot, 1), :]
            src, dst = (tok_row, slot_row) if to_slots else (slot_row, tok_row)
            pltpu.make_async_copy(src, dst, sem).start(priority=j % 2)
        return carry
    lax.fori_loop(0, PB // SUBLANES, body, 0)
    pltpu.make_async_copy(tok_ref, tok_ref, sem).wait()


def _dispatch_kernel(zoff_ref, zflag_ref, slot_ref, tok_ref, rows_hbm, zeros_ref, sem, zsem):
    @pl.when(pl.program_id(0) == 0)
    def _():
        zeros_ref[...] = jnp.zeros(zeros_ref.shape, zeros_ref.dtype)
        for b in range(zflag_ref.shape[0]):
            @pl.when(zflag_ref[b] != 0)
            def _(b=b):
                off = pl.multiple_of(zoff_ref[b], TM)
                pltpu.make_async_copy(zeros_ref, rows_hbm.at[pl.ds(off, TM), :], zsem).start()
        for b in range(zflag_ref.shape[0]):
            @pl.when(zflag_ref[b] != 0)
            def _(b=b):
                off = pl.multiple_of(zoff_ref[b], TM)
                pltpu.make_async_copy(zeros_ref, rows_hbm.at[pl.ds(off, TM), :], zsem).wait()

    _issue_row_copies(slot_ref, tok_ref, rows_hbm, sem, to_slots=True)


def _combine_kernel(slot_ref, rows_hbm, tok_ref, sem):
    _issue_row_copies(slot_ref, tok_ref, rows_hbm, sem, to_slots=False)


def _slot_block_spec(index_map):
    return pl.BlockSpec((None, 1, PB), index_map, memory_space=pltpu.SMEM)


def _dispatch_call(zoff, zflag, slots, hext, n_slots):
    n_tokens = hext.shape[0]
    grid_spec = pltpu.PrefetchScalarGridSpec(
        num_scalar_prefetch=2,
        grid=(n_tokens // PB,),
        in_specs=[
            _slot_block_spec(lambda i, zo, zf: (i, 0, 0)),
            pl.BlockSpec((PB // SUBLANES, SUBLANES, HEXT), lambda i, zo, zf: (i, 0, 0)),
        ],
        out_specs=pl.BlockSpec(memory_space=pl.ANY),
        scratch_shapes=[
            pltpu.VMEM((TM, HEXT), F32),
            pltpu.SemaphoreType.DMA(()),
            pltpu.SemaphoreType.DMA(()),
        ],
    )
    return pl.pallas_call(
        _dispatch_kernel,
        grid_spec=grid_spec,
        out_shape=jax.ShapeDtypeStruct((n_slots, HEXT), F32),
        compiler_params=pltpu.CompilerParams(dimension_semantics=("arbitrary",),
                                             vmem_limit_bytes=VMEM_LIMIT),
        name="dispatch",
    )(zoff, zflag, slots, hext.reshape(n_tokens // SUBLANES, SUBLANES, HEXT))


def _combine_call(slots, res, n_tokens):
    out = pl.pallas_call(
        _combine_kernel,
        grid=(n_tokens // PB,),
        in_specs=[
            _slot_block_spec(lambda i: (i, 0, 0)),
            pl.BlockSpec(memory_space=pl.ANY),
        ],
        out_specs=pl.BlockSpec((PB // SUBLANES, SUBLANES, D_MODEL), lambda i: (i, 0, 0)),
        out_shape=jax.ShapeDtypeStruct((n_tokens // SUBLANES, SUBLANES, D_MODEL), F32),
        scratch_shapes=[pltpu.SemaphoreType.DMA(())],
        compiler_params=pltpu.CompilerParams(dimension_semantics=("arbitrary",),
                                             vmem_limit_bytes=VMEM_LIMIT),
        name="combine",
    )(slots, res)
    return out.reshape(n_tokens, D_MODEL)


def _moe_kernel(ea_ref, eb_ref, na_ref,
                rows_ref, wga_ref, wua_ref, wda_ref, wgb_ref, wub_ref, wdb_ref,
                nfw_ref, fnw_ref,
                out_ref, wgu_a, wd_a, wgu_b, wd_b):
    i = pl.program_id(0)
    n_act = na_ref[0]
    prev = jnp.maximum(i - 1, 0)

    @pl.when((i == 0) | (ea_ref[i] != ea_ref[prev]))
    def _():
        wgu_a[:, 0:EXPERT_HIDDEN] = wga_ref[...].astype(BF16)
        wgu_a[:, EXPERT_HIDDEN:2 * EXPERT_HIDDEN] = wua_ref[...].astype(BF16)
        wd_a[...] = wda_ref[...].astype(BF16)

    @pl.when((i == 0) | (eb_ref[i] != eb_ref[prev]))
    def _():
        wgu_b[:, 0:EXPERT_HIDDEN] = wgb_ref[...].astype(BF16)
        wgu_b[:, EXPERT_HIDDEN:2 * EXPERT_HIDDEN] = wub_ref[...].astype(BF16)
        wd_b[...] = wdb_ref[...].astype(BF16)

    @pl.when(i < n_act)
    def _():
        rows = rows_ref[...]
        hrow = rows[:, 0:D_MODEL]
        wa = rows[:, D_MODEL + 3:D_MODEL + 4]
        wb = rows[:, D_MODEL + 4:D_MODEL + 5]
        xn = _rms(hrow, nfw_ref[...]).astype(BF16)

        def expert(wgu, wd):
            gu = _dot(xn, wgu[...])
            gate = gu[:, 0:EXPERT_HIDDEN]
            hid = (gate * jax.nn.sigmoid(gate)) * gu[:, EXPERT_HIDDEN:2 * EXPERT_HIDDEN]
            return _dot(hid.astype(BF16), wd[...])

        y = wa * expert(wgu_a, wd_a)
        y = y + wb * expert(wgu_b, wd_b)
        out_ref[...] = _rms(hrow + y, fnw_ref[...])

    @pl.when(i >= n_act)
    def _():
        out_ref[...] = jnp.zeros(out_ref.shape, out_ref.dtype)


def _moe_call(ea, eb, nact, rows, wg, wu, wd, nfw, fnw):
    n_tiles = rows.shape[0] // TM
    sel_a = lambda i, ea, eb, na: (ea[i], 0, 0)
    sel_b = lambda i, ea, eb, na: (eb[i], 0, 0)
    tile = lambda i, ea, eb, na: (jnp.minimum(i, na[0] - 1), 0)
    up_spec = lambda sel: pl.BlockSpec((None, D_MODEL, EXPERT_HIDDEN), sel)
    down_spec = lambda sel: pl.BlockSpec((None, EXPERT_HIDDEN, D_MODEL), sel)
    grid_spec = pltpu.PrefetchScalarGridSpec(
        num_scalar_prefetch=3,
        grid=(n_tiles,),
        in_specs=[
            pl.BlockSpec((TM, HEXT), tile),
            up_spec(sel_a), up_spec(sel_a), down_spec(sel_a),
            up_spec(sel_b), up_spec(sel_b), down_spec(sel_b),
            pl.BlockSpec((1, D_MODEL), lambda i, ea, eb, na: (0, 0)),
            pl.BlockSpec((1, D_MODEL), lambda i, ea, eb, na: (0, 0)),
        ],
        out_specs=pl.BlockSpec((TM, D_MODEL), lambda i, ea, eb, na: (i, 0)),
        scratch_shapes=[
            pltpu.VMEM((D_MODEL, 2 * EXPERT_HIDDEN), BF16),
            pltpu.VMEM((EXPERT_HIDDEN, D_MODEL), BF16),
            pltpu.VMEM((D_MODEL, 2 * EXPERT_HIDDEN), BF16),
            pltpu.VMEM((EXPERT_HIDDEN, D_MODEL), BF16),
        ],
    )
    return pl.pallas_call(
        _moe_kernel,
        grid_spec=grid_spec,
        out_shape=jax.ShapeDtypeStruct((n_tiles * TM, D_MODEL), F32),
        compiler_params=pltpu.CompilerParams(
            dimension_semantics=("arbitrary",),
            vmem_limit_bytes=VMEM_LIMIT),
        name="moe",
    )(ea, eb, nact, rows, wg, wu, wd, wg, wu, wd, nfw, fnw)


def _slot_tables(route, n_tokens):
    g = route[:, 0, :].reshape(n_tokens).astype(I32)
    ja = route[:, 1, :].reshape(n_tokens).astype(I32)
    jb = route[:, 2, :].reshape(n_tokens).astype(I32)
    pair = (ja * (7 - ja)) // 2 + (jb - ja - 1)
    bucket = g * N_PAIR_BUCKETS + pair
    onehot = (bucket[:, None] == jnp.arange(N_BUCKETS, dtype=I32)[None, :]).astype(I32)
    csum = jnp.cumsum(onehot, axis=0)
    counts = csum[-1]
    rank = jnp.sum(csum * onehot, axis=1) - 1
    padded = ((counts + TM - 1) // TM) * TM
    ends = jnp.cumsum(padded)
    starts = ends - padded
    dst = jnp.take(starts, bucket) + rank
    n_tiles = n_tokens // TM + N_BUCKETS
    tile_start = jnp.arange(n_tiles, dtype=I32) * TM
    total = ends[-1]
    nact = total // TM
    tb = jnp.minimum(jnp.sum((tile_start[:, None] >= ends[None, :]).astype(I32), axis=1), N_BUCKETS - 1)
    tb = jnp.where(tile_start < total, tb, jnp.take(tb, jnp.maximum(nact - 1, 0)))
    pair_a = jnp.array([0, 0, 0, 1, 1, 2], I32)
    pair_b = jnp.array([1, 2, 3, 2, 3, 3], I32)
    ea = (tb // N_PAIR_BUCKETS) * EXPERTS_PER_GROUP + jnp.take(pair_a, tb % N_PAIR_BUCKETS)
    eb = (tb // N_PAIR_BUCKETS) * EXPERTS_PER_GROUP + jnp.take(pair_b, tb % N_PAIR_BUCKETS)
    tail = nact + jnp.arange(N_BUCKETS, dtype=I32)
    zflag = jnp.concatenate([(counts % TM != 0).astype(I32), (tail < n_tiles).astype(I32)])
    zoff = jnp.concatenate([jnp.maximum(ends - TM, 0), jnp.minimum(tail, n_tiles - 1) * TM])
    return (ea, eb, jnp.reshape(nact, (1,)).astype(I32), zoff.astype(I32), zflag,
            dst.reshape(n_tokens // PB, 1, PB))


def kernel(x, meta_tokens, norm_mix_w, w_in, w_gate_up, b_gate, gla_norm_w, pool_w, pool_scale, w_out, norm_ffn_w, w_router_group, b_router_group, w_router_expert, b_router_expert, w_expert_gate, w_expert_up, w_expert_down, final_norm_w):
    bsz, seq, d = x.shape
    assert d == D_MODEL and seq % TN == 0 and bsz % SEQS == 0 and (bsz * seq) % TM == 0
    assert norm_mix_w.shape[0] == 1
    n_tokens = bsz * seq

    wgu_gate = jnp.pad(w_gate_up[0], ((0, LANES - GATE_RANK), (0, 0))).astype(BF16)
    w_r = jnp.pad(jnp.concatenate([w_router_group[0], w_router_expert[0]], axis=1).T,
                  ((0, ROUTE_ROWS - N_GROUPS - N_EXPERTS), (0, 0)))
    w_r_hi = w_r.astype(BF16)
    w_r_lo = (w_r - w_r_hi.astype(F32)).astype(BF16)
    b_r = jnp.pad(jnp.concatenate([b_router_group[0], b_router_expert[0]]),
                  (0, ROUTE_ROWS - N_GROUPS - N_EXPERTS))
    b_r = jnp.broadcast_to(b_r[:, None], (ROUTE_ROWS, TN))

    hext, route = _mixer_call(
        x, meta_tokens, norm_mix_w[0].reshape(1, d), w_in[0], wgu_gate, b_gate[0].reshape(1, GLA_KDIM),
        gla_norm_w[0].reshape(1, GLA_HEAD_V), pool_w[0].astype(BF16), pool_scale[0].reshape(1, POOL_WIDTH),
        w_out[0], norm_ffn_w[0].reshape(1, d), jnp.concatenate([w_r_hi, w_r_lo], axis=0), b_r)

    ea, eb, nact, zoff, zflag, slots = _slot_tables(route, n_tokens)
    n_slots = n_tokens + N_BUCKETS * TM
    rows = _dispatch_call(zoff, zflag, slots, hext.reshape(n_tokens, HEXT), n_slots)
    res = _moe_call(ea, eb, nact, rows, w_expert_gate[0], w_expert_up[0], w_expert_down[0],
                    norm_ffn_w[0].reshape(1, d), final_norm_w.reshape(1, d))
    return _combine_call(slots, res, n_tokens).reshape(bsz, seq, d)
```

```python
import jax
import jax.numpy as jnp
from jax import lax
from jax.experimental import pallas as pl
from jax.experimental.pallas import tpu as pltpu

F32 = jnp.float32
BF16 = jnp.bfloat16
I32 = jnp.int32

D_MODEL = 1024
N_META = 16
POOL_WIDTH = 512
POOL_WINDOWS = (2, 4, 8, 16)
POOL_GROUP = 128
GLA_HEADS = 4
GLA_VDIM = 512
GLA_HEAD_V = 128
GLA_KDIM = 256
GLA_HEAD_K = 64
GATE_RANK = 16
GATE_NORMALIZER = 16.0
CHUNK = 64
N_GROUPS = 4
EXPERTS_PER_GROUP = 4
N_EXPERTS = 16
EXPERT_HIDDEN = 256
EPS = 1e-6

LANES = 128
SUBLANES = 8
IN_COLS = POOL_WIDTH + 2 * GLA_KDIM + 2 * GLA_VDIM + GATE_RANK
IN_PAD = POOL_WIDTH + 2 * GLA_KDIM + 2 * GLA_VDIM + LANES
C_Q = POOL_WIDTH
C_K = C_Q + GLA_KDIM
C_V = C_K + GLA_KDIM
C_R = C_V + GLA_VDIM
C_G = C_R + GLA_VDIM
D_TILES = D_MODEL // LANES
REC = D_TILES + 1
ROUTE_ROWS = 32

TN = 256
SEQS = 4
N_CHUNKS = TN // CHUNK
N_PAIRS = GLA_HEADS // 2
HIST = 16

TM = 256
PB = 2048
ISSUE_UNROLL = 8
N_PAIR_BUCKETS = 6
N_BUCKETS = N_GROUPS * N_PAIR_BUCKETS
VMEM_LIMIT = 56 * 1024 * 1024


def _dot(a, b):
    return jnp.dot(a, b, preferred_element_type=F32)


def _dot_nt(a, b):
    return lax.dot_general(a, b, (((1,), (1,)), ((), ())), preferred_element_type=F32)


def _dot_tn(a, b):
    return lax.dot_general(a, b, (((0,), (0,)), ((), ())), preferred_element_type=F32)


def _rms(x, w):
    return x * lax.rsqrt(jnp.mean(x * x, axis=-1, keepdims=True) + EPS) * w


def _log_sigmoid(z):
    return -(jnp.maximum(-z, 0.0) + jnp.log1p(jnp.exp(-jnp.abs(z))))


def _seg_cumsum(x, seg):
    pos = lax.broadcasted_iota(I32, x.shape, 0) % seg
    s = 1
    while s < seg:
        x = x + jnp.where(pos >= s, pltpu.roll(x, s, axis=0), 0.0)
        s *= 2
    return x


def _log_decay(g, wgu_ref, bg_ref):
    z = _dot(g.astype(BF16), wgu_ref[...]) + bg_ref[...]
    return _log_sigmoid(z) / GATE_NORMALIZER


def _state_mask():
    r = lax.broadcasted_iota(I32, (2 * GLA_HEAD_V, 2 * GLA_HEAD_K), 0) // GLA_HEAD_V
    c = lax.broadcasted_iota(I32, (2 * GLA_HEAD_V, 2 * GLA_HEAD_K), 1) // GLA_HEAD_K
    return r == c


def _first_index(vals, target):
    idx = jnp.full_like(target, float(len(vals) - 1))
    for j in range(len(vals) - 2, -1, -1):
        idx = jnp.where(vals[j] == target, float(j), idx)
    return idx


def _max_list(vals):
    m = vals[0]
    for val in vals[1:]:
        m = jnp.maximum(m, val)
    return m


def _route(lt):
    row = lambda i: lt[i:i + 1, :]
    gl = [row(i) for i in range(N_GROUPS)]
    gmax = _max_list(gl)
    gsel = _first_index(gl, gmax)
    gsum = gl[0] * 0.0
    for val in gl:
        gsum = gsum + jnp.exp(val - gmax)
    p_g = 1.0 / gsum
    el = []
    for j in range(EXPERTS_PER_GROUP):
        val = row(N_GROUPS + EXPERTS_PER_GROUP * (N_GROUPS - 1) + j)
        for g in range(N_GROUPS - 2, -1, -1):
            val = jnp.where(gsel == float(g), row(N_GROUPS + EXPERTS_PER_GROUP * g + j), val)
        el.append(val)
    emax = _max_list(el)
    ee = [jnp.exp(val - emax) for val in el]
    esum = ee[0]
    for val in ee[1:]:
        esum = esum + val
    ep = [val / esum for val in ee]
    p1 = _max_list(ep)
    j1 = _first_index(ep, p1)
    ep_rest = [jnp.where(j1 == float(j), -1.0, ep[j]) for j in range(EXPERTS_PER_GROUP)]
    p2 = _max_list(ep_rest)
    j2 = _first_index(ep_rest, p2)
    den = p1 + p2
    w1 = p_g * (p1 / den)
    w2 = p_g * (p2 / den)
    first_low = j1 < j2
    ja = jnp.where(first_low, j1, j2)
    jb = jnp.where(first_low, j2, j1)
    wa = jnp.where(first_low, w1, w2)
    wb = jnp.where(first_low, w2, w1)
    zero = jnp.zeros_like(wa)
    return jnp.concatenate([gsel, ja, jb, wa, wb, zero, zero, zero], axis=0)


def _store_records(rec_ref, rows, last_row):
    n = rows.shape[0]
    for j in range(D_TILES):
        rec_ref[pl.ds(j, n, stride=REC), :] = rows[:, LANES * j:LANES * (j + 1)]
    rec_ref[pl.ds(D_TILES, n, stride=REC), :] = last_row


def _load_records(rec_ref, n):
    rows = jnp.concatenate([rec_ref[pl.ds(j, n, stride=REC), :] for j in range(D_TILES)], axis=1)
    return rows, rec_ref[pl.ds(D_TILES, n, stride=REC), :]


def _mixer_tile(x, pvh_ref, st_ref, nmw_ref, win_ref, wgu_ref, bg_ref, gnw_ref, pw_ref, ps_ref,
                wout_ref, nfw_ref, wr_ref, br_ref, rec_out, route_out):
    smask = _state_mask()
    u = _rms(x, nmw_ref[...]).astype(BF16)
    proj = _dot(u, win_ref[...])
    yield

    pv = proj[:, 0:POOL_WIDTH]
    pvh_ref[HIST:HIST + TN, :] = pv
    ext = pvh_ref[...]
    sums = [ext[:, POOL_GROUP * gi:POOL_GROUP * (gi + 1)] for gi in range(len(POOL_WINDOWS))]
    shift = 1
    for stage in range(len(POOL_WINDOWS)):
        for gi in range(stage, len(POOL_WINDOWS)):
            sums[gi] = sums[gi] + pltpu.roll(sums[gi], shift, axis=0)
        shift *= 2
    mixed = []
    for gi, w in enumerate(POOL_WINDOWS):
        cs = slice(POOL_GROUP * gi, POOL_GROUP * (gi + 1))
        pooled = sums[gi][HIST:HIST + TN, :] * (1.0 / w) - pv[:, cs]
        mixed.append(_dot(pooled.astype(BF16), pw_ref[gi]))
    y_pool = jnp.concatenate(mixed, axis=1) * ps_ref[...]
    pvh_ref[0:HIST, :] = ext[TN:TN + HIST, :]

    q = proj[:, C_Q:C_K]
    k = proj[:, C_K:C_V]
    v = proj[:, C_V:C_R].astype(BF16)
    r = proj[:, C_R:C_G]
    la = _log_decay(proj[:, C_G:IN_PAD], wgu_ref, bg_ref)
    b = _seg_cumsum(la, CHUNK)
    b3 = b.reshape(N_CHUNKS, CHUNK, GLA_KDIM)
    bl = b3[:, CHUNK - 1:CHUNK, :]
    blb = jnp.broadcast_to(bl, (N_CHUNKS, CHUNK, GLA_KDIM)).reshape(TN, GLA_KDIM)
    q_t = ((q * (GLA_HEAD_K ** -0.5)) * jnp.exp(b)).astype(BF16)
    k_t = (k * jnp.exp(-b)).astype(BF16)
    k_end = (k * jnp.exp(blb - b)).astype(BF16)
    dec = jnp.exp(bl)
    yield

    ri = lax.broadcasted_iota(I32, (TN, TN), 0)
    ci = lax.broadcasted_iota(I32, (TN, TN), 1)
    causal = (ri // CHUNK == ci // CHUNK) & (ci <= ri)
    lane_k = lax.broadcasted_iota(I32, (TN, LANES), 1) // GLA_HEAD_K

    o_heads = [None] * GLA_HEADS
    for p in range(N_PAIRS):
        ks = slice(LANES * p, LANES * (p + 1))
        vs = slice(2 * GLA_HEAD_V * p, 2 * GLA_HEAD_V * (p + 1))
        qp = q_t[:, ks]
        kp = k_t[:, ks]
        o_inter = []
        st = st_ref[p]
        for c in range(N_CHUNKS):
            rs = slice(CHUNK * c, CHUNK * (c + 1))
            o_inter.append(_dot_nt(qp[rs], st.astype(BF16)))
            ds = _dot_tn(v[rs, vs], k_end[rs, ks])
            st = st * dec[c][:, ks] + jnp.where(smask, ds, 0.0)
        st_ref[p] = st
        o_inter = jnp.concatenate(o_inter, axis=0)
        for hh in range(2):
            head = 2 * p + hh
            qm = jnp.where(lane_k == hh, qp, jnp.zeros_like(qp))
            sc = jnp.where(causal, _dot_nt(qm, kp), 0.0).astype(BF16)
            hv = slice(GLA_HEAD_V * head, GLA_HEAD_V * (head + 1))
            o_heads[head] = _dot(sc, v[:, hv]) + o_inter[:, GLA_HEAD_V * hh:GLA_HEAD_V * (hh + 1)]

    gnw = gnw_ref[...]
    y_heads = []
    for head in range(GLA_HEADS):
        o = o_heads[head]
        y_heads.append(o * lax.rsqrt(jnp.mean(o * o, axis=-1, keepdims=True) + EPS) * gnw)
    y_gla = jnp.concatenate(y_heads, axis=1) * (r * jax.nn.sigmoid(r))
    yield

    ycat = jnp.concatenate([y_pool, y_gla], axis=1).astype(BF16)
    h = x + _dot(ycat, wout_ref[...])
    yield

    xn = _rms(h, nfw_ref[...])
    xh = xn.astype(BF16)
    xl = (xn - xh.astype(F32)).astype(BF16)
    ph = _dot_nt(wr_ref[...], xh)
    plo = _dot_nt(wr_ref[...], xl)
    lt = ((ph[0:ROUTE_ROWS] + ph[ROUTE_ROWS:2 * ROUTE_ROWS])
          + (plo[0:ROUTE_ROWS] + plo[ROUTE_ROWS:2 * ROUTE_ROWS])) + br_ref[...]
    route = _route(lt)
    route_out[...] = route
    _store_records(rec_out, h, jnp.concatenate([route, jnp.zeros((LANES - 8, TN), F32)], axis=0).T)


def _mixer_kernel(x_ref, meta_ref, nmw_ref, winf_ref, wgu_ref, bg_ref, gnw_ref, pw_ref, ps_ref,
                  woutf_ref, nfw_ref, wr_ref, br_ref,
                  hext_ref, route_ref, win_ref, wout_ref, pvh_ref, st_ref):
    bi = pl.program_id(0)
    t = pl.program_id(1)

    @pl.when((bi == 0) & (t == 0))
    def _():
        win_ref[:, 0:C_G] = winf_ref[:, 0:C_G].astype(BF16)
        win_ref[:, C_G:IN_PAD] = jnp.zeros((D_MODEL, LANES), BF16)
        win_ref[:, C_G:C_G + GATE_RANK] = winf_ref[:, C_G:IN_COLS].astype(BF16)
        wout_ref[...] = woutf_ref[...].astype(BF16)

    @pl.when(t == 0)
    def _():
        smask = _state_mask()
        um = _rms(meta_ref[...], nmw_ref[...]).astype(BF16)
        pm = _dot(um, win_ref[...])
        la = _log_decay(pm[:, C_G:IN_PAD], wgu_ref, bg_ref)
        b = _seg_cumsum(la, N_META)
        k_end = (pm[:, C_K:C_V] * jnp.exp(b[N_META - 1:N_META, :] - b)).astype(BF16)
        vm = pm[:, C_V:C_R].astype(BF16)
        for p in range(N_PAIRS):
            ds = _dot_tn(vm[:, 2 * GLA_HEAD_V * p:2 * GLA_HEAD_V * (p + 1)],
                         k_end[:, LANES * p:LANES * (p + 1)])
            ds = jnp.where(smask, ds, 0.0)
            for s in range(SEQS):
                st_ref[s, p] = ds
        for s in range(SEQS):
            pvh_ref[s, 0:HIST, :] = pm[:, 0:POOL_WIDTH]

    tiles = [
        _mixer_tile(x_ref[s], pvh_ref.at[s], st_ref.at[s], nmw_ref, win_ref, wgu_ref, bg_ref, gnw_ref,
                    pw_ref, ps_ref, wout_ref, nfw_ref, wr_ref, br_ref,
                    hext_ref.at[s], route_ref.at[s])
        for s in range(SEQS)]
    live = list(range(SEQS))
    step = 0
    while live:
        for s in list(live):
            if step >= s and next(tiles[s], "done") == "done":
                live.remove(s)
        step += 1


def _mixer_call(x, meta, nmw, winf, wgu, bg, gnw, pw, ps, woutf, nfw, wr, br):
    bsz, seq, _ = x.shape

    def full(shape, single=False):
        kw = dict(pipeline_mode=pl.Buffered(1)) if single else {}
        return pl.BlockSpec(shape, lambda b, t: (0,) * len(shape), **kw)

    return pl.pallas_call(
        _mixer_kernel,
        grid=(bsz // SEQS, seq // TN),
        in_specs=[
            pl.BlockSpec((SEQS, TN, D_MODEL), lambda b, t: (b, t, 0)),
            full((N_META, D_MODEL)),
            full((1, D_MODEL)),
            full((D_MODEL, IN_COLS), single=True),
            full((LANES, GLA_KDIM)),
            full((1, GLA_KDIM)),
            full((1, GLA_HEAD_V)),
            full((len(POOL_WINDOWS), POOL_GROUP, POOL_GROUP)),
            full((1, POOL_WIDTH)),
            full((D_MODEL, D_MODEL), single=True),
            full((1, D_MODEL)),
            full((2 * ROUTE_ROWS, D_MODEL)),
            full((ROUTE_ROWS, TN)),
        ],
        out_specs=[
            pl.BlockSpec((SEQS, TN * REC, LANES), lambda b, t: (b, t, 0)),
            pl.BlockSpec((SEQS, 8, TN), lambda b, t: (b, 0, t)),
        ],
        out_shape=[
            jax.ShapeDtypeStruct((bsz, seq * REC, LANES), F32),
            jax.ShapeDtypeStruct((bsz, 8, seq), F32),
        ],
        scratch_shapes=[
            pltpu.VMEM((D_MODEL, IN_PAD), BF16),
            pltpu.VMEM((D_MODEL, D_MODEL), BF16),
            pltpu.VMEM((SEQS, HIST + TN, POOL_WIDTH), F32),
            pltpu.VMEM((SEQS, N_PAIRS, 2 * GLA_HEAD_V, 2 * GLA_HEAD_K), F32),
        ],
        compiler_params=pltpu.CompilerParams(
            dimension_semantics=("arbitrary", "arbitrary"),
            vmem_limit_bytes=VMEM_LIMIT),
        name="mixer",
    )(x, meta, nmw, winf, wgu, bg, gnw, pw, ps, woutf, nfw, wr, br)


def _copy_records(slot_ref, tok_ref, slot_hbm, sem, to_slots):
    def body(gi, carry):
        for j in range(ISSUE_UNROLL):
            r = gi * ISSUE_UNROLL + j
            tok_rec = tok_ref.at[pl.ds(r * REC, REC), :]
            slot_rec = slot_hbm.at[pl.ds(slot_ref[0, r] * REC, REC), :]
            src, dst = (tok_rec, slot_rec) if to_slots else (slot_rec, tok_rec)
            pltpu.make_async_copy(src, dst, sem).start(priority=j % 2)
        return carry
    lax.fori_loop(0, PB // ISSUE_UNROLL, body, 0)
    pltpu.make_async_copy(tok_ref, tok_ref, sem).wait()


def _dispatch_kernel(zoff_ref, zflag_ref, slot_ref, tok_ref, rows_hbm, zeros_ref, sem, zsem):
    @pl.when(pl.program_id(0) == 0)
    def _():
        zeros_ref[...] = jnp.zeros(zeros_ref.shape, zeros_ref.dtype)
        for b in range(zflag_ref.shape[0]):
            @pl.when(zflag_ref[b] != 0)
            def _(b=b):
                off = pl.multiple_of(zoff_ref[b], TM * REC)
                pltpu.make_async_copy(zeros_ref, rows_hbm.at[pl.ds(off, TM * REC), :], zsem).start()
        for b in range(zflag_ref.shape[0]):
            @pl.when(zflag_ref[b] != 0)
            def _(b=b):
                off = pl.multiple_of(zoff_ref[b], TM * REC)
                pltpu.make_async_copy(zeros_ref, rows_hbm.at[pl.ds(off, TM * REC), :], zsem).wait()

    _copy_records(slot_ref, tok_ref, rows_hbm, sem, to_slots=True)


def _combine_kernel(slot_ref, rows_hbm, out_ref, stage_ref, sem):
    _copy_records(slot_ref, stage_ref, rows_hbm, sem, to_slots=False)
    for j in range(D_TILES):
        out_ref[:, LANES * j:LANES * (j + 1)] = stage_ref[pl.ds(j, PB, stride=REC), :]


def _slot_block_spec(index_map):
    return pl.BlockSpec((None, 1, PB), index_map, memory_space=pltpu.SMEM)


def _dispatch_call(zoff, zflag, slots, recs, n_slots):
    n_tokens = recs.shape[0] // REC
    grid_spec = pltpu.PrefetchScalarGridSpec(
        num_scalar_prefetch=2,
        grid=(n_tokens // PB,),
        in_specs=[
            _slot_block_spec(lambda i, zo, zf: (i, 0, 0)),
            pl.BlockSpec((PB * REC, LANES), lambda i, zo, zf: (i, 0)),
        ],
        out_specs=pl.BlockSpec(memory_space=pl.ANY),
        scratch_shapes=[
            pltpu.VMEM((TM * REC, LANES), F32),
            pltpu.SemaphoreType.DMA(()),
            pltpu.SemaphoreType.DMA(()),
        ],
    )
    return pl.pallas_call(
        _dispatch_kernel,
        grid_spec=grid_spec,
        out_shape=jax.ShapeDtypeStruct((n_slots * REC, LANES), F32),
        compiler_params=pltpu.CompilerParams(dimension_semantics=("arbitrary",),
                                             vmem_limit_bytes=VMEM_LIMIT),
        name="dispatch",
    )(zoff, zflag, slots, recs)


def _combine_call(slots, res, n_tokens):
    return pl.pallas_call(
        _combine_kernel,
        grid=(n_tokens // PB,),
        in_specs=[
            _slot_block_spec(lambda i: (i, 0, 0)),
            pl.BlockSpec(memory_space=pl.ANY),
        ],
        out_specs=pl.BlockSpec((PB, D_MODEL), lambda i: (i, 0)),
        out_shape=jax.ShapeDtypeStruct((n_tokens, D_MODEL), F32),
        scratch_shapes=[pltpu.VMEM((PB * REC, LANES), F32), pltpu.SemaphoreType.DMA(())],
        compiler_params=pltpu.CompilerParams(dimension_semantics=("arbitrary",),
                                             vmem_limit_bytes=VMEM_LIMIT),
        name="combine",
    )(slots, res)


def _moe_kernel(ea_ref, eb_ref, na_ref,
                rows_ref, wga_ref, wua_ref, wda_ref, wgb_ref, wub_ref, wdb_ref,
                nfw_ref, fnw_ref,
                out_ref, wgu_a, wd_a, wgu_b, wd_b):
    i = pl.program_id(0)
    n_act = na_ref[0]
    prev = jnp.maximum(i - 1, 0)

    @pl.when((i == 0) | (ea_ref[i] != ea_ref[prev]))
    def _():
        wgu_a[:, 0:EXPERT_HIDDEN] = wga_ref[...].astype(BF16)
        wgu_a[:, EXPERT_HIDDEN:2 * EXPERT_HIDDEN] = wua_ref[...].astype(BF16)
        wd_a[...] = wda_ref[...].astype(BF16)

    @pl.when((i == 0) | (eb_ref[i] != eb_ref[prev]))
    def _():
        wgu_b[:, 0:EXPERT_HIDDEN] = wgb_ref[...].astype(BF16)
        wgu_b[:, EXPERT_HIDDEN:2 * EXPERT_HIDDEN] = wub_ref[...].astype(BF16)
        wd_b[...] = wdb_ref[...].astype(BF16)

    @pl.when(i < n_act)
    def _():
        hrow, slab = _load_records(rows_ref, TM)
        wa = slab[:, 3:4]
        wb = slab[:, 4:5]
        xn = _rms(hrow, nfw_ref[...]).astype(BF16)

        def expert(wgu, wd):
            gu = _dot(xn, wgu[...])
            gate = gu[:, 0:EXPERT_HIDDEN]
            hid = (gate * jax.nn.sigmoid(gate)) * gu[:, EXPERT_HIDDEN:2 * EXPERT_HIDDEN]
            return _dot(hid.astype(BF16), wd[...])

        y = wa * expert(wgu_a, wd_a)
        y = y + wb * expert(wgu_b, wd_b)
        _store_records(out_ref, _rms(hrow + y, fnw_ref[...]), jnp.zeros((TM, LANES), F32))

    @pl.when(i >= n_act)
    def _():
        out_ref[...] = jnp.zeros(out_ref.shape, out_ref.dtype)


def _moe_call(ea, eb, nact, rows, wg, wu, wd, nfw, fnw):
    n_tiles = rows.shape[0] // (TM * REC)
    sel_a = lambda i, ea, eb, na: (ea[i], 0, 0)
    sel_b = lambda i, ea, eb, na: (eb[i], 0, 0)
    tile = lambda i, ea, eb, na: (jnp.minimum(i, na[0] - 1), 0)
    up_spec = lambda sel: pl.BlockSpec((None, D_MODEL, EXPERT_HIDDEN), sel)
    down_spec = lambda sel: pl.BlockSpec((None, EXPERT_HIDDEN, D_MODEL), sel)
    grid_spec = pltpu.PrefetchScalarGridSpec(
        num_scalar_prefetch=3,
        grid=(n_tiles,),
        in_specs=[
            pl.BlockSpec((TM * REC, LANES), tile),
            up_spec(sel_a), up_spec(sel_a), down_spec(sel_a),
            up_spec(sel_b), up_spec(sel_b), down_spec(sel_b),
            pl.BlockSpec((1, D_MODEL), lambda i, ea, eb, na: (0, 0)),
            pl.BlockSpec((1, D_MODEL), lambda i, ea, eb, na: (0, 0)),
        ],
        out_specs=pl.BlockSpec((TM * REC, LANES), lambda i, ea, eb, na: (i, 0)),
        scratch_shapes=[
            pltpu.VMEM((D_MODEL, 2 * EXPERT_HIDDEN), BF16),
            pltpu.VMEM((EXPERT_HIDDEN, D_MODEL), BF16),
            pltpu.VMEM((D_MODEL, 2 * EXPERT_HIDDEN), BF16),
            pltpu.VMEM((EXPERT_HIDDEN, D_MODEL), BF16),
        ],
    )
    return pl.pallas_call(
        _moe_kernel,
        grid_spec=grid_spec,
        out_shape=jax.ShapeDtypeStruct((n_tiles * TM * REC, LANES), F32),
        compiler_params=pltpu.CompilerParams(
            dimension_semantics=("arbitrary",),
            vmem_limit_bytes=VMEM_LIMIT),
        name="moe",
    )(ea, eb, nact, rows, wg, wu, wd, wg, wu, wd, nfw, fnw)


def _slot_tables(route, n_tokens):
    g = route[:, 0, :].reshape(n_tokens).astype(I32)
    ja = route[:, 1, :].reshape(n_tokens).astype(I32)
    jb = route[:, 2, :].reshape(n_tokens).astype(I32)
    pair = (ja * (7 - ja)) // 2 + (jb - ja - 1)
    bucket = g * N_PAIR_BUCKETS + pair
    onehot = (bucket[:, None] == jnp.arange(N_BUCKETS, dtype=I32)[None, :]).astype(I32)
    csum = jnp.cumsum(onehot, axis=0)
    counts = csum[-1]
    rank = jnp.sum(csum * onehot, axis=1) - 1
    padded = ((counts + TM - 1) // TM) * TM
    ends = jnp.cumsum(padded)
    starts = ends - padded
    dst = jnp.take(starts, bucket) + rank
    n_tiles = n_tokens // TM + N_BUCKETS
    tile_start = jnp.arange(n_tiles, dtype=I32) * TM
    total = ends[-1]
    nact = total // TM
    tb = jnp.minimum(jnp.sum((tile_start[:, None] >= ends[None, :]).astype(I32), axis=1), N_BUCKETS - 1)
    tb = jnp.where(tile_start < total, tb, jnp.take(tb, jnp.maximum(nact - 1, 0)))
    pair_a = jnp.array([0, 0, 0, 1, 1, 2], I32)
    pair_b = jnp.array([1, 2, 3, 2, 3, 3], I32)
    ea = (tb // N_PAIR_BUCKETS) * EXPERTS_PER_GROUP + jnp.take(pair_a, tb % N_PAIR_BUCKETS)
    eb = (tb // N_PAIR_BUCKETS) * EXPERTS_PER_GROUP + jnp.take(pair_b, tb % N_PAIR_BUCKETS)
    tail = nact + jnp.arange(N_BUCKETS, dtype=I32)
    zflag = jnp.concatenate([(counts % TM != 0).astype(I32), (tail < n_tiles).astype(I32)])
    zoff = jnp.concatenate([jnp.maximum(ends - TM, 0), jnp.minimum(tail, n_tiles - 1) * TM]) * REC
    return (ea, eb, jnp.reshape(nact, (1,)).astype(I32), zoff.astype(I32), zflag,
            dst.reshape(n_tokens // PB, 1, PB))


def kernel(x, meta_tokens, norm_mix_w, w_in, w_gate_up, b_gate, gla_norm_w, pool_w, pool_scale, w_out, norm_ffn_w, w_router_group, b_router_group, w_router_expert, b_router_expert, w_expert_gate, w_expert_up, w_expert_down, final_norm_w):
    bsz, seq, d = x.shape
    assert d == D_MODEL and seq % TN == 0 and bsz % SEQS == 0 and (bsz * seq) % TM == 0
    assert norm_mix_w.shape[0] == 1
    n_tokens = bsz * seq

    wgu_gate = jnp.pad(w_gate_up[0], ((0, LANES - GATE_RANK), (0, 0))).astype(BF16)
    w_r = jnp.pad(jnp.concatenate([w_router_group[0], w_router_expert[0]], axis=1).T,
                  ((0, ROUTE_ROWS - N_GROUPS - N_EXPERTS), (0, 0)))
    w_r_hi = w_r.astype(BF16)
    w_r_lo = (w_r - w_r_hi.astype(F32)).astype(BF16)
    b_r = jnp.pad(jnp.concatenate([b_router_group[0], b_router_expert[0]]),
                  (0, ROUTE_ROWS - N_GROUPS - N_EXPERTS))
    b_r = jnp.broadcast_to(b_r[:, None], (ROUTE_ROWS, TN))

    hext, route = _mixer_call(
        x, meta_tokens, norm_mix_w[0].reshape(1, d), w_in[0], wgu_gate, b_gate[0].reshape(1, GLA_KDIM),
        gla_norm_w[0].reshape(1, GLA_HEAD_V), pool_w[0].astype(BF16), pool_scale[0].reshape(1, POOL_WIDTH),
        w_out[0], norm_ffn_w[0].reshape(1, d), jnp.concatenate([w_r_hi, w_r_lo], axis=0), b_r)

    ea, eb, nact, zoff, zflag, slots = _slot_tables(route, n_tokens)
    n_slots = n_tokens + N_BUCKETS * TM
    rows = _dispatch_call(zoff, zflag, slots, hext.reshape(n_tokens * REC, LANES), n_slots)
    res = _moe_call(ea, eb, nact, rows, w_expert_gate[0], w_expert_up[0], w_expert_down[0],
                    norm_ffn_w[0].reshape(1, d), final_norm_w.reshape(1, d))
    return _combine_call(slots, res, n_tokens).reshape(bsz, seq, d)
```

```python
import jax
import jax.numpy as jnp
from jax import lax
from jax.experimental import pallas as pl
from jax.experimental.pallas import tpu as pltpu

F32 = jnp.float32
BF16 = jnp.bfloat16
I32 = jnp.int32

D_MODEL = 1024
N_META = 16
POOL_WIDTH = 512
POOL_WINDOWS = (2, 4, 8, 16)
POOL_GROUP = 128
GLA_HEADS = 4
GLA_VDIM = 512
GLA_HEAD_V = 128
GLA_KDIM = 256
GLA_HEAD_K = 64
GATE_RANK = 16
GATE_NORMALIZER = 16.0
CHUNK = 64
N_GROUPS = 4
EXPERTS_PER_GROUP = 4
N_EXPERTS = 16
EXPERT_HIDDEN = 256
EPS = 1e-6

LANES = 128
SUBLANES = 8
IN_COLS = POOL_WIDTH + 2 * GLA_KDIM + 2 * GLA_VDIM + GATE_RANK
IN_PAD = POOL_WIDTH + 2 * GLA_KDIM + 2 * GLA_VDIM + LANES
C_Q = POOL_WIDTH
C_K = C_Q + GLA_KDIM
C_V = C_K + GLA_KDIM
C_R = C_V + GLA_VDIM
C_G = C_R + GLA_VDIM
D_TILES = D_MODEL // LANES
REC = D_TILES + 1
ROUTE_ROWS = 32

TN = 256
SEQS = 4
N_CHUNKS = TN // CHUNK
N_PAIRS = GLA_HEADS // 2
HIST = 16

TM = 512
PB = 2048
ISSUE_UNROLL = 8
N_PAIR_BUCKETS = 6
N_BUCKETS = N_GROUPS * N_PAIR_BUCKETS
VMEM_LIMIT = 56 * 1024 * 1024


def _dot(a, b):
    return jnp.dot(a, b, preferred_element_type=F32)


def _dot_nt(a, b):
    return lax.dot_general(a, b, (((1,), (1,)), ((), ())), preferred_element_type=F32)


def _dot_tn(a, b):
    return lax.dot_general(a, b, (((0,), (0,)), ((), ())), preferred_element_type=F32)


def _rms(x, w):
    return x * lax.rsqrt(jnp.mean(x * x, axis=-1, keepdims=True) + EPS) * w


def _log_sigmoid(z):
    return -(jnp.maximum(-z, 0.0) + jnp.log1p(jnp.exp(-jnp.abs(z))))


def _seg_cumsum(x, seg):
    pos = lax.broadcasted_iota(I32, x.shape, 0) % seg
    s = 1
    while s < seg:
        x = x + jnp.where(pos >= s, pltpu.roll(x, s, axis=0), 0.0)
        s *= 2
    return x


def _log_decay(g, wgu_ref, bg_ref):
    z = _dot(g.astype(BF16), wgu_ref[...]) + bg_ref[...]
    return _log_sigmoid(z) / GATE_NORMALIZER


def _state_mask():
    r = lax.broadcasted_iota(I32, (2 * GLA_HEAD_V, 2 * GLA_HEAD_K), 0) // GLA_HEAD_V
    c = lax.broadcasted_iota(I32, (2 * GLA_HEAD_V, 2 * GLA_HEAD_K), 1) // GLA_HEAD_K
    return r == c


def _first_index(vals, target):
    idx = jnp.full_like(target, float(len(vals) - 1))
    for j in range(len(vals) - 2, -1, -1):
        idx = jnp.where(vals[j] == target, float(j), idx)
    return idx


def _max_list(vals):
    m = vals[0]
    for val in vals[1:]:
        m = jnp.maximum(m, val)
    return m


def _route(lt):
    row = lambda i: lt[i:i + 1, :]
    gl = [row(i) for i in range(N_GROUPS)]
    gmax = _max_list(gl)
    gsel = _first_index(gl, gmax)
    gsum = gl[0] * 0.0
    for val in gl:
        gsum = gsum + jnp.exp(val - gmax)
    p_g = 1.0 / gsum
    el = []
    for j in range(EXPERTS_PER_GROUP):
        val = row(N_GROUPS + EXPERTS_PER_GROUP * (N_GROUPS - 1) + j)
        for g in range(N_GROUPS - 2, -1, -1):
            val = jnp.where(gsel == float(g), row(N_GROUPS + EXPERTS_PER_GROUP * g + j), val)
        el.append(val)
    emax = _max_list(el)
    ee = [jnp.exp(val - emax) for val in el]
    esum = ee[0]
    for val in ee[1:]:
        esum = esum + val
    ep = [val / esum for val in ee]
    p1 = _max_list(ep)
    j1 = _first_index(ep, p1)
    ep_rest = [jnp.where(j1 == float(j), -1.0, ep[j]) for j in range(EXPERTS_PER_GROUP)]
    p2 = _max_list(ep_rest)
    j2 = _first_index(ep_rest, p2)
    den = p1 + p2
    w1 = p_g * (p1 / den)
    w2 = p_g * (p2 / den)
    first_low = j1 < j2
    ja = jnp.where(first_low, j1, j2)
    jb = jnp.where(first_low, j2, j1)
    wa = jnp.where(first_low, w1, w2)
    wb = jnp.where(first_low, w2, w1)
    zero = jnp.zeros_like(wa)
    return jnp.concatenate([gsel, ja, jb, wa, wb, zero, zero, zero], axis=0)


def _store_records(rec_ref, rows, last_row):
    n = rows.shape[0]
    for j in range(D_TILES):
        rec_ref[pl.ds(j, n, stride=REC), :] = rows[:, LANES * j:LANES * (j + 1)]
    rec_ref[pl.ds(D_TILES, n, stride=REC), :] = last_row


def _load_records(rec_ref, n):
    rows = jnp.concatenate([rec_ref[pl.ds(j, n, stride=REC), :] for j in range(D_TILES)], axis=1)
    return rows, rec_ref[pl.ds(D_TILES, n, stride=REC), :]


def _mixer_tile(x, pvh_ref, st_ref, nmw_ref, win_ref, wgu_ref, bg_ref, gnw_ref, pw_ref, ps_ref,
                wout_ref, nfw_ref, wr_ref, br_ref, rec_out, route_out):
    smask = _state_mask()
    u = _rms(x, nmw_ref[...]).astype(BF16)
    proj = _dot(u, win_ref[...])
    yield

    pv = proj[:, 0:POOL_WIDTH]
    pvh_ref[HIST:HIST + TN, :] = pv
    ext = pvh_ref[...]
    sums = [ext[:, POOL_GROUP * gi:POOL_GROUP * (gi + 1)] for gi in range(len(POOL_WINDOWS))]
    shift = 1
    for stage in range(len(POOL_WINDOWS)):
        for gi in range(stage, len(POOL_WINDOWS)):
            sums[gi] = sums[gi] + pltpu.roll(sums[gi], shift, axis=0)
        shift *= 2
    mixed = []
    for gi, w in enumerate(POOL_WINDOWS):
        cs = slice(POOL_GROUP * gi, POOL_GROUP * (gi + 1))
        pooled = sums[gi][HIST:HIST + TN, :] * (1.0 / w) - pv[:, cs]
        mixed.append(_dot(pooled.astype(BF16), pw_ref[gi]))
    y_pool = jnp.concatenate(mixed, axis=1) * ps_ref[...]
    pvh_ref[0:HIST, :] = ext[TN:TN + HIST, :]

    q = proj[:, C_Q:C_K]
    k = proj[:, C_K:C_V]
    v = proj[:, C_V:C_R].astype(BF16)
    r = proj[:, C_R:C_G]
    la = _log_decay(proj[:, C_G:IN_PAD], wgu_ref, bg_ref)
    b = _seg_cumsum(la, CHUNK)
    b3 = b.reshape(N_CHUNKS, CHUNK, GLA_KDIM)
    bl = b3[:, CHUNK - 1:CHUNK, :]
    blb = jnp.broadcast_to(bl, (N_CHUNKS, CHUNK, GLA_KDIM)).reshape(TN, GLA_KDIM)
    q_t = ((q * (GLA_HEAD_K ** -0.5)) * jnp.exp(b)).astype(BF16)
    k_t = (k * jnp.exp(-b)).astype(BF16)
    k_end = (k * jnp.exp(blb - b)).astype(BF16)
    dec = jnp.exp(bl)
    yield

    ri = lax.broadcasted_iota(I32, (TN, TN), 0)
    ci = lax.broadcasted_iota(I32, (TN, TN), 1)
    causal = (ri // CHUNK == ci // CHUNK) & (ci <= ri)
    lane_k = lax.broadcasted_iota(I32, (TN, LANES), 1) // GLA_HEAD_K

    o_heads = [None] * GLA_HEADS
    for p in range(N_PAIRS):
        ks = slice(LANES * p, LANES * (p + 1))
        vs = slice(2 * GLA_HEAD_V * p, 2 * GLA_HEAD_V * (p + 1))
        qp = q_t[:, ks]
        kp = k_t[:, ks]
        o_inter = []
        st = st_ref[p]
        for c in range(N_CHUNKS):
            rs = slice(CHUNK * c, CHUNK * (c + 1))
            o_inter.append(_dot_nt(qp[rs], st.astype(BF16)))
            ds = _dot_tn(v[rs, vs], k_end[rs, ks])
            st = st * dec[c][:, ks] + jnp.where(smask, ds, 0.0)
        st_ref[p] = st
        o_inter = jnp.concatenate(o_inter, axis=0)
        for hh in range(2):
            head = 2 * p + hh
            qm = jnp.where(lane_k == hh, qp, jnp.zeros_like(qp))
            sc = jnp.where(causal, _dot_nt(qm, kp), 0.0).astype(BF16)
            hv = slice(GLA_HEAD_V * head, GLA_HEAD_V * (head + 1))
            o_heads[head] = _dot(sc, v[:, hv]) + o_inter[:, GLA_HEAD_V * hh:GLA_HEAD_V * (hh + 1)]

    gnw = gnw_ref[...]
    y_heads = []
    for head in range(GLA_HEADS):
        o = o_heads[head]
        y_heads.append(o * lax.rsqrt(jnp.mean(o * o, axis=-1, keepdims=True) + EPS) * gnw)
    y_gla = jnp.concatenate(y_heads, axis=1) * (r * jax.nn.sigmoid(r))
    yield

    ycat = jnp.concatenate([y_pool, y_gla], axis=1).astype(BF16)
    h = x + _dot(ycat, wout_ref[...])
    yield

    xn = _rms(h, nfw_ref[...])
    xh = xn.astype(BF16)
    xl = (xn - xh.astype(F32)).astype(BF16)
    ph = _dot_nt(wr_ref[...], xh)
    plo = _dot_nt(wr_ref[...], xl)
    lt = ((ph[0:ROUTE_ROWS] + ph[ROUTE_ROWS:2 * ROUTE_ROWS])
          + (plo[0:ROUTE_ROWS] + plo[ROUTE_ROWS:2 * ROUTE_ROWS])) + br_ref[...]
    route = _route(lt)
    route_out[...] = route
    _store_records(rec_out, h, jnp.concatenate([route, jnp.zeros((LANES - 8, TN), F32)], axis=0).T)


def _mixer_kernel(x_ref, meta_ref, nmw_ref, winf_ref, wgu_ref, bg_ref, gnw_ref, pw_ref, ps_ref,
                  woutf_ref, nfw_ref, wr_ref, br_ref,
                  hext_ref, route_ref, win_ref, wout_ref, pvh_ref, st_ref):
    bi = pl.program_id(0)
    t = pl.program_id(1)

    @pl.when((bi == 0) & (t == 0))
    def _():
        win_ref[:, 0:C_G] = winf_ref[:, 0:C_G].astype(BF16)
        win_ref[:, C_G:IN_PAD] = jnp.zeros((D_MODEL, LANES), BF16)
        win_ref[:, C_G:C_G + GATE_RANK] = winf_ref[:, C_G:IN_COLS].astype(BF16)
        wout_ref[...] = woutf_ref[...].astype(BF16)

    @pl.when(t == 0)
    def _():
        smask = _state_mask()
        um = _rms(meta_ref[...], nmw_ref[...]).astype(BF16)
        pm = _dot(um, win_ref[...])
        la = _log_decay(pm[:, C_G:IN_PAD], wgu_ref, bg_ref)
        b = _seg_cumsum(la, N_META)
        k_end = (pm[:, C_K:C_V] * jnp.exp(b[N_META - 1:N_META, :] - b)).astype(BF16)
        vm = pm[:, C_V:C_R].astype(BF16)
        for p in range(N_PAIRS):
            ds = _dot_tn(vm[:, 2 * GLA_HEAD_V * p:2 * GLA_HEAD_V * (p + 1)],
                         k_end[:, LANES * p:LANES * (p + 1)])
            ds = jnp.where(smask, ds, 0.0)
            for s in range(SEQS):
                st_ref[s, p] = ds
        for s in range(SEQS):
            pvh_ref[s, 0:HIST, :] = pm[:, 0:POOL_WIDTH]

    tiles = [
        _mixer_tile(x_ref[s], pvh_ref.at[s], st_ref.at[s], nmw_ref, win_ref, wgu_ref, bg_ref, gnw_ref,
                    pw_ref, ps_ref, wout_ref, nfw_ref, wr_ref, br_ref,
                    hext_ref.at[s], route_ref.at[s])
        for s in range(SEQS)]
    live = list(range(SEQS))
    step = 0
    while live:
        for s in list(live):
            if step >= s and next(tiles[s], "done") == "done":
                live.remove(s)
        step += 1


def _mixer_call(x, meta, nmw, winf, wgu, bg, gnw, pw, ps, woutf, nfw, wr, br):
    bsz, seq, _ = x.shape

    def full(shape, single=False):
        kw = dict(pipeline_mode=pl.Buffered(1)) if single else {}
        return pl.BlockSpec(shape, lambda b, t: (0,) * len(shape), **kw)

    return pl.pallas_call(
        _mixer_kernel,
        grid=(bsz // SEQS, seq // TN),
        in_specs=[
            pl.BlockSpec((SEQS, TN, D_MODEL), lambda b, t: (b, t, 0)),
            full((N_META, D_MODEL)),
            full((1, D_MODEL)),
            full((None, D_MODEL, IN_COLS), single=True),
            full((LANES, GLA_KDIM)),
            full((1, GLA_KDIM)),
            full((1, GLA_HEAD_V)),
            full((len(POOL_WINDOWS), POOL_GROUP, POOL_GROUP)),
            full((1, POOL_WIDTH)),
            full((None, D_MODEL, D_MODEL), single=True),
            full((1, D_MODEL)),
            full((2 * ROUTE_ROWS, D_MODEL)),
            full((ROUTE_ROWS, TN)),
        ],
        out_specs=[
            pl.BlockSpec((SEQS, TN * REC, LANES), lambda b, t: (b, t, 0)),
            pl.BlockSpec((SEQS, 8, TN), lambda b, t: (b, 0, t)),
        ],
        out_shape=[
            jax.ShapeDtypeStruct((bsz, seq * REC, LANES), F32),
            jax.ShapeDtypeStruct((bsz, 8, seq), F32),
        ],
        scratch_shapes=[
            pltpu.VMEM((D_MODEL, IN_PAD), BF16),
            pltpu.VMEM((D_MODEL, D_MODEL), BF16),
            pltpu.VMEM((SEQS, HIST + TN, POOL_WIDTH), F32),
            pltpu.VMEM((SEQS, N_PAIRS, 2 * GLA_HEAD_V, 2 * GLA_HEAD_K), F32),
        ],
        compiler_params=pltpu.CompilerParams(
            dimension_semantics=("arbitrary", "arbitrary"),
            vmem_limit_bytes=VMEM_LIMIT),
        name="mixer",
    )(x, meta, nmw, winf, wgu, bg, gnw, pw, ps, woutf, nfw, wr, br)


def _copy_records(slot_ref, tok_ref, slot_hbm, sem, to_slots):
    def body(gi, carry):
        for j in range(ISSUE_UNROLL):
            r = gi * ISSUE_UNROLL + j
            tok_rec = tok_ref.at[pl.ds(r * REC, REC), :]
            slot_rec = slot_hbm.at[pl.ds(slot_ref[0, r] * REC, REC), :]
            src, dst = (tok_rec, slot_rec) if to_slots else (slot_rec, tok_rec)
            pltpu.make_async_copy(src, dst, sem).start(priority=j % 2)
        return carry
    lax.fori_loop(0, PB // ISSUE_UNROLL, body, 0)
    pltpu.make_async_copy(tok_ref, tok_ref, sem).wait()


def _dispatch_kernel(zoff_ref, zflag_ref, slot_ref, tok_ref, rows_hbm, zeros_ref, sem, zsem):
    @pl.when(pl.program_id(0) == 0)
    def _():
        zeros_ref[...] = jnp.zeros(zeros_ref.shape, zeros_ref.dtype)
        for b in range(zflag_ref.shape[0]):
            @pl.when(zflag_ref[b] != 0)
            def _(b=b):
                off = pl.multiple_of(zoff_ref[b], TM * REC)
                pltpu.make_async_copy(zeros_ref, rows_hbm.at[pl.ds(off, TM * REC), :], zsem).start()
        for b in range(zflag_ref.shape[0]):
            @pl.when(zflag_ref[b] != 0)
            def _(b=b):
                off = pl.multiple_of(zoff_ref[b], TM * REC)
                pltpu.make_async_copy(zeros_ref, rows_hbm.at[pl.ds(off, TM * REC), :], zsem).wait()

    _copy_records(slot_ref, tok_ref, rows_hbm, sem, to_slots=True)


def _combine_kernel(slot_ref, rows_hbm, out_ref, stage_ref, sem):
    _copy_records(slot_ref, stage_ref, rows_hbm, sem, to_slots=False)
    for j in range(D_TILES):
        out_ref[:, LANES * j:LANES * (j + 1)] = stage_ref[pl.ds(j, PB, stride=REC), :]


def _slot_block_spec(index_map):
    return pl.BlockSpec((None, 1, PB), index_map, memory_space=pltpu.SMEM)


def _dispatch_call(zoff, zflag, slots, recs, n_slots):
    n_tokens = recs.shape[0] // REC
    grid_spec = pltpu.PrefetchScalarGridSpec(
        num_scalar_prefetch=2,
        grid=(n_tokens // PB,),
        in_specs=[
            _slot_block_spec(lambda i, zo, zf: (i, 0, 0)),
            pl.BlockSpec((PB * REC, LANES), lambda i, zo, zf: (i, 0)),
        ],
        out_specs=pl.BlockSpec(memory_space=pl.ANY),
        scratch_shapes=[
            pltpu.VMEM((TM * REC, LANES), F32),
            pltpu.SemaphoreType.DMA(()),
            pltpu.SemaphoreType.DMA(()),
        ],
    )
    return pl.pallas_call(
        _dispatch_kernel,
        grid_spec=grid_spec,
        out_shape=jax.ShapeDtypeStruct((n_slots * REC, LANES), F32),
        compiler_params=pltpu.CompilerParams(dimension_semantics=("arbitrary",),
                                             vmem_limit_bytes=VMEM_LIMIT),
        name="dispatch",
    )(zoff, zflag, slots, recs)


def _combine_call(slots, res, n_tokens):
    return pl.pallas_call(
        _combine_kernel,
        grid=(n_tokens // PB,),
        in_specs=[
            _slot_block_spec(lambda i: (i, 0, 0)),
            pl.BlockSpec(memory_space=pl.ANY),
        ],
        out_specs=pl.BlockSpec((PB, D_MODEL), lambda i: (i, 0)),
        out_shape=jax.ShapeDtypeStruct((n_tokens, D_MODEL), F32),
        scratch_shapes=[pltpu.VMEM((PB * REC, LANES), F32), pltpu.SemaphoreType.DMA(())],
        compiler_params=pltpu.CompilerParams(dimension_semantics=("arbitrary",),
                                             vmem_limit_bytes=VMEM_LIMIT),
        name="combine",
    )(slots, res)


def _moe_kernel(ea_ref, eb_ref, na_ref,
                rows_ref, wga_ref, wua_ref, wda_ref, wgb_ref, wub_ref, wdb_ref,
                nfw_ref, fnw_ref,
                out_ref, wgu_a, wd_a, wgu_b, wd_b):
    i = pl.program_id(0)
    n_act = na_ref[0]
    prev = jnp.maximum(i - 1, 0)

    @pl.when((i == 0) | (ea_ref[i] != ea_ref[prev]))
    def _():
        wgu_a[:, 0:EXPERT_HIDDEN] = wga_ref[...].astype(BF16)
        wgu_a[:, EXPERT_HIDDEN:2 * EXPERT_HIDDEN] = wua_ref[...].astype(BF16)
        wd_a[...] = wda_ref[...].astype(BF16)

    @pl.when((i == 0) | (eb_ref[i] != eb_ref[prev]))
    def _():
        wgu_b[:, 0:EXPERT_HIDDEN] = wgb_ref[...].astype(BF16)
        wgu_b[:, EXPERT_HIDDEN:2 * EXPERT_HIDDEN] = wub_ref[...].astype(BF16)
        wd_b[...] = wdb_ref[...].astype(BF16)

    @pl.when(i < n_act)
    def _():
        hrow, slab = _load_records(rows_ref, TM)
        wa = slab[:, 3:4]
        wb = slab[:, 4:5]
        xn = _rms(hrow, nfw_ref[...]).astype(BF16)

        def expert(wgu, wd):
            gu = _dot(xn, wgu[...])
            gate = gu[:, 0:EXPERT_HIDDEN]
            hid = (gate * jax.nn.sigmoid(gate)) * gu[:, EXPERT_HIDDEN:2 * EXPERT_HIDDEN]
            return _dot(hid.astype(BF16), wd[...])

        y = wa * expert(wgu_a, wd_a)
        y = y + wb * expert(wgu_b, wd_b)
        _store_records(out_ref, _rms(hrow + y, fnw_ref[...]), jnp.zeros((TM, LANES), F32))

    @pl.when(i >= n_act)
    def _():
        out_ref[...] = jnp.zeros(out_ref.shape, out_ref.dtype)


def _moe_call(ea, eb, nact, rows, wg, wu, wd, nfw, fnw):
    n_tiles = rows.shape[0] // (TM * REC)
    sel_a = lambda i, ea, eb, na: (ea[i], 0, 0)
    sel_b = lambda i, ea, eb, na: (eb[i], 0, 0)
    tile = lambda i, ea, eb, na: (jnp.minimum(i, na[0] - 1), 0)
    up_spec = lambda sel: pl.BlockSpec((None, D_MODEL, EXPERT_HIDDEN), sel)
    down_spec = lambda sel: pl.BlockSpec((None, EXPERT_HIDDEN, D_MODEL), sel)
    grid_spec = pltpu.PrefetchScalarGridSpec(
        num_scalar_prefetch=3,
        grid=(n_tiles,),
        in_specs=[
            pl.BlockSpec((TM * REC, LANES), tile),
            up_spec(sel_a), up_spec(sel_a), down_spec(sel_a),
            up_spec(sel_b), up_spec(sel_b), down_spec(sel_b),
            pl.BlockSpec((1, D_MODEL), lambda i, ea, eb, na: (0, 0)),
            pl.BlockSpec((1, D_MODEL), lambda i, ea, eb, na: (0, 0)),
        ],
        out_specs=pl.BlockSpec((TM * REC, LANES), lambda i, ea, eb, na: (i, 0)),
        scratch_shapes=[
            pltpu.VMEM((D_MODEL, 2 * EXPERT_HIDDEN), BF16),
            pltpu.VMEM((EXPERT_HIDDEN, D_MODEL), BF16),
            pltpu.VMEM((D_MODEL, 2 * EXPERT_HIDDEN), BF16),
            pltpu.VMEM((EXPERT_HIDDEN, D_MODEL), BF16),
        ],
    )
    return pl.pallas_call(
        _moe_kernel,
        grid_spec=grid_spec,
        out_shape=jax.ShapeDtypeStruct((n_tiles * TM * REC, LANES), F32),
        compiler_params=pltpu.CompilerParams(
            dimension_semantics=("arbitrary",),
            vmem_limit_bytes=VMEM_LIMIT),
        name="moe",
    )(ea, eb, nact, rows, wg, wu, wd, wg, wu, wd, nfw, fnw)


def _slot_tables(route, n_tokens):
    g = route[:, 0, :].reshape(n_tokens).astype(I32)
    ja = route[:, 1, :].reshape(n_tokens).astype(I32)
    jb = route[:, 2, :].reshape(n_tokens).astype(I32)
    pair = (ja * (7 - ja)) // 2 + (jb - ja - 1)
    bucket = g * N_PAIR_BUCKETS + pair
    onehot = (bucket[:, None] == jnp.arange(N_BUCKETS, dtype=I32)[None, :]).astype(I32)
    csum = jnp.cumsum(onehot, axis=0)
    counts = csum[-1]
    rank = jnp.sum(csum * onehot, axis=1) - 1
    padded = ((counts + TM - 1) // TM) * TM
    ends = jnp.cumsum(padded)
    starts = ends - padded
    dst = jnp.take(starts, bucket) + rank
    n_tiles = n_tokens // TM + N_BUCKETS
    tile_start = jnp.arange(n_tiles, dtype=I32) * TM
    total = ends[-1]
    nact = total // TM
    tb = jnp.minimum(jnp.sum((tile_start[:, None] >= ends[None, :]).astype(I32), axis=1), N_BUCKETS - 1)
    tb = jnp.where(tile_start < total, tb, jnp.take(tb, jnp.maximum(nact - 1, 0)))
    pair_a = jnp.array([0, 0, 0, 1, 1, 2], I32)
    pair_b = jnp.array([1, 2, 3, 2, 3, 3], I32)
    ea = (tb // N_PAIR_BUCKETS) * EXPERTS_PER_GROUP + jnp.take(pair_a, tb % N_PAIR_BUCKETS)
    eb = (tb // N_PAIR_BUCKETS) * EXPERTS_PER_GROUP + jnp.take(pair_b, tb % N_PAIR_BUCKETS)
    tail = nact + jnp.arange(N_BUCKETS, dtype=I32)
    zflag = jnp.concatenate([(counts % TM != 0).astype(I32), (tail < n_tiles).astype(I32)])
    zoff = jnp.concatenate([jnp.maximum(ends - TM, 0), jnp.minimum(tail, n_tiles - 1) * TM]) * REC
    return (ea, eb, jnp.reshape(nact, (1,)).astype(I32), zoff.astype(I32), zflag,
            dst.reshape(n_tokens // PB, 1, PB))


def kernel(x, meta_tokens, norm_mix_w, w_in, w_gate_up, b_gate, gla_norm_w, pool_w, pool_scale, w_out, norm_ffn_w, w_router_group, b_router_group, w_router_expert, b_router_expert, w_expert_gate, w_expert_up, w_expert_down, final_norm_w):
    bsz, seq, d = x.shape
    assert d == D_MODEL and seq % TN == 0 and bsz % SEQS == 0 and (bsz * seq) % TM == 0
    assert norm_mix_w.shape[0] == 1
    n_tokens = bsz * seq

    wgu_gate = jnp.pad(w_gate_up[0], ((0, LANES - GATE_RANK), (0, 0))).astype(BF16)
    w_r = jnp.pad(jnp.concatenate([w_router_group[0], w_router_expert[0]], axis=1).T,
                  ((0, ROUTE_ROWS - N_GROUPS - N_EXPERTS), (0, 0)))
    w_r_hi = w_r.astype(BF16)
    w_r_lo = (w_r - w_r_hi.astype(F32)).astype(BF16)
    b_r = jnp.pad(jnp.concatenate([b_router_group[0], b_router_expert[0]]),
                  (0, ROUTE_ROWS - N_GROUPS - N_EXPERTS))
    b_r = jnp.broadcast_to(b_r[:, None], (ROUTE_ROWS, TN))

    hext, route = _mixer_call(
        x, meta_tokens, norm_mix_w[0].reshape(1, d), w_in, wgu_gate, b_gate[0].reshape(1, GLA_KDIM),
        gla_norm_w[0].reshape(1, GLA_HEAD_V), pool_w[0].astype(BF16), pool_scale[0].reshape(1, POOL_WIDTH),
        w_out, norm_ffn_w[0].reshape(1, d), jnp.concatenate([w_r_hi, w_r_lo], axis=0), b_r)

    ea, eb, nact, zoff, zflag, slots = _slot_tables(route, n_tokens)
    n_slots = n_tokens + N_BUCKETS * TM
    rows = _dispatch_call(zoff, zflag, slots, hext.reshape(n_tokens * REC, LANES), n_slots)
    res = _moe_call(ea, eb, nact, rows, w_expert_gate[0], w_expert_up[0], w_expert_down[0],
                    norm_ffn_w[0].reshape(1, d), final_norm_w.reshape(1, d))
    return _combine_call(slots, res, n_tokens).reshape(bsz, seq, d)
```

```python
import jax
import jax.numpy as jnp
from jax import lax
from jax.experimental import pallas as pl
from jax.experimental.pallas import tpu as pltpu

F32 = jnp.float32
BF16 = jnp.bfloat16
I32 = jnp.int32

D_MODEL = 1024
N_META = 16
POOL_WIDTH = 512
POOL_WINDOWS = (2, 4, 8, 16)
POOL_GROUP = 128
GLA_HEADS = 4
GLA_VDIM = 512
GLA_HEAD_V = 128
GLA_KDIM = 256
GLA_HEAD_K = 64
GATE_RANK = 16
GATE_NORMALIZER = 16.0
CHUNK = 64
N_GROUPS = 4
EXPERTS_PER_GROUP = 4
N_EXPERTS = 16
EXPERT_HIDDEN = 256
EPS = 1e-6

LANES = 128
SUBLANES = 8
IN_COLS = POOL_WIDTH + 2 * GLA_KDIM + 2 * GLA_VDIM + GATE_RANK
IN_PAD = POOL_WIDTH + 2 * GLA_KDIM + 2 * GLA_VDIM + LANES
C_Q = POOL_WIDTH
C_K = C_Q + GLA_KDIM
C_V = C_K + GLA_KDIM
C_R = C_V + GLA_VDIM
C_G = C_R + GLA_VDIM
D_TILES = D_MODEL // LANES
REC = D_TILES + 1
ROUTE_ROWS = 32

TN = 256
SEQS = 4
N_CHUNKS = TN // CHUNK
N_PAIRS = GLA_HEADS // 2
HIST = 16

MIXER_LAG = 1
TM = 512
MOE_PARTS = 1
PB_DISPATCH = 4096
PB_COMBINE = 2048
ISSUE_UNROLL = 8
N_PAIR_BUCKETS = 6
N_BUCKETS = N_GROUPS * N_PAIR_BUCKETS
VMEM_LIMIT = 56 * 1024 * 1024


def _dot(a, b):
    return jnp.dot(a, b, preferred_element_type=F32)


def _dot_nt(a, b):
    return lax.dot_general(a, b, (((1,), (1,)), ((), ())), preferred_element_type=F32)


def _dot_tn(a, b):
    return lax.dot_general(a, b, (((0,), (0,)), ((), ())), preferred_element_type=F32)


def _rms(x, w):
    return x * lax.rsqrt(jnp.mean(x * x, axis=-1, keepdims=True) + EPS) * w


def _log_sigmoid(z):
    return -(jnp.maximum(-z, 0.0) + jnp.log1p(jnp.exp(-jnp.abs(z))))


def _seg_cumsum(x, seg):
    pos = lax.broadcasted_iota(I32, x.shape, 0) % seg
    s = 1
    while s < seg:
        x = x + jnp.where(pos >= s, pltpu.roll(x, s, axis=0), 0.0)
        s *= 2
    return x


def _log_decay(g, wgu_ref, bg_ref):
    z = _dot(g.astype(BF16), wgu_ref[...]) + bg_ref[...]
    return _log_sigmoid(z) / GATE_NORMALIZER


def _state_mask():
    r = lax.broadcasted_iota(I32, (2 * GLA_HEAD_V, 2 * GLA_HEAD_K), 0) // GLA_HEAD_V
    c = lax.broadcasted_iota(I32, (2 * GLA_HEAD_V, 2 * GLA_HEAD_K), 1) // GLA_HEAD_K
    return r == c


def _first_index(vals, target):
    idx = jnp.full_like(target, float(len(vals) - 1))
    for j in range(len(vals) - 2, -1, -1):
        idx = jnp.where(vals[j] == target, float(j), idx)
    return idx


def _max_list(vals):
    m = vals[0]
    for val in vals[1:]:
        m = jnp.maximum(m, val)
    return m


def _route(lt):
    row = lambda i: lt[i:i + 1, :]
    gl = [row(i) for i in range(N_GROUPS)]
    gmax = _max_list(gl)
    gsel = _first_index(gl, gmax)
    gsum = gl[0] * 0.0
    for val in gl:
        gsum = gsum + jnp.exp(val - gmax)
    p_g = 1.0 / gsum
    el = []
    for j in range(EXPERTS_PER_GROUP):
        val = row(N_GROUPS + EXPERTS_PER_GROUP * (N_GROUPS - 1) + j)
        for g in range(N_GROUPS - 2, -1, -1):
            val = jnp.where(gsel == float(g), row(N_GROUPS + EXPERTS_PER_GROUP * g + j), val)
        el.append(val)
    emax = _max_list(el)
    ee = [jnp.exp(val - emax) for val in el]
    esum = ee[0]
    for val in ee[1:]:
        esum = esum + val
    ep = [val / esum for val in ee]
    p1 = _max_list(ep)
    j1 = _first_index(ep, p1)
    ep_rest = [jnp.where(j1 == float(j), -1.0, ep[j]) for j in range(EXPERTS_PER_GROUP)]
    p2 = _max_list(ep_rest)
    j2 = _first_index(ep_rest, p2)
    den = p1 + p2
    w1 = p_g * (p1 / den)
    w2 = p_g * (p2 / den)
    first_low = j1 < j2
    ja = jnp.where(first_low, j1, j2)
    jb = jnp.where(first_low, j2, j1)
    wa = jnp.where(first_low, w1, w2)
    wb = jnp.where(first_low, w2, w1)
    zero = jnp.zeros_like(wa)
    return jnp.concatenate([gsel, ja, jb, wa, wb, zero, zero, zero], axis=0)


def _store_records(rec_ref, rows, last_row, first=0):
    n = rows.shape[0]
    for j in range(D_TILES):
        rec_ref[pl.ds(first * REC + j, n, stride=REC), :] = rows[:, LANES * j:LANES * (j + 1)]
    rec_ref[pl.ds(first * REC + D_TILES, n, stride=REC), :] = last_row


def _load_records(rec_ref, n, first=0):
    rows = jnp.concatenate(
        [rec_ref[pl.ds(first * REC + j, n, stride=REC), :] for j in range(D_TILES)], axis=1)
    return rows, rec_ref[pl.ds(first * REC + D_TILES, n, stride=REC), :]


def _run_staggered(parts, lag=1):
    live = list(range(len(parts)))
    step = 0
    while live:
        for k in list(live):
            if step >= k * lag and next(parts[k], "done") == "done":
                live.remove(k)
        step += 1


def _mixer_tile(x, pvh_ref, st_ref, nmw_ref, win_ref, wgu_ref, bg_ref, gnw_ref, pw_ref, ps_ref,
                wout_ref, nfw_ref, wr_ref, br_ref, rec_out, route_out):
    smask = _state_mask()
    u = _rms(x, nmw_ref[...]).astype(BF16)
    proj = _dot(u, win_ref[...])
    yield

    pv = proj[:, 0:POOL_WIDTH]
    pvh_ref[HIST:HIST + TN, :] = pv
    ext = pvh_ref[...]
    sums = [ext[:, POOL_GROUP * gi:POOL_GROUP * (gi + 1)] for gi in range(len(POOL_WINDOWS))]
    shift = 1
    for stage in range(len(POOL_WINDOWS)):
        for gi in range(stage, len(POOL_WINDOWS)):
            sums[gi] = sums[gi] + pltpu.roll(sums[gi], shift, axis=0)
        shift *= 2
    mixed = []
    for gi, w in enumerate(POOL_WINDOWS):
        cs = slice(POOL_GROUP * gi, POOL_GROUP * (gi + 1))
        pooled = sums[gi][HIST:HIST + TN, :] * (1.0 / w) - pv[:, cs]
        mixed.append(_dot(pooled.astype(BF16), pw_ref[gi]))
    y_pool = jnp.concatenate(mixed, axis=1) * ps_ref[...]
    pvh_ref[0:HIST, :] = ext[TN:TN + HIST, :]

    q = proj[:, C_Q:C_K]
    k = proj[:, C_K:C_V]
    v = proj[:, C_V:C_R].astype(BF16)
    r = proj[:, C_R:C_G]
    la = _log_decay(proj[:, C_G:IN_PAD], wgu_ref, bg_ref)
    b = _seg_cumsum(la, CHUNK)
    b3 = b.reshape(N_CHUNKS, CHUNK, GLA_KDIM)
    bl = b3[:, CHUNK - 1:CHUNK, :]
    blb = jnp.broadcast_to(bl, (N_CHUNKS, CHUNK, GLA_KDIM)).reshape(TN, GLA_KDIM)
    q_t = ((q * (GLA_HEAD_K ** -0.5)) * jnp.exp(b)).astype(BF16)
    k_t = (k * jnp.exp(-b)).astype(BF16)
    k_end = (k * jnp.exp(blb - b)).astype(BF16)
    dec = jnp.exp(bl)
    yield

    ri = lax.broadcasted_iota(I32, (TN, TN), 0)
    ci = lax.broadcasted_iota(I32, (TN, TN), 1)
    causal = (ri // CHUNK == ci // CHUNK) & (ci <= ri)
    lane_k = lax.broadcasted_iota(I32, (TN, LANES), 1) // GLA_HEAD_K

    o_heads = [None] * GLA_HEADS
    for p in range(N_PAIRS):
        ks = slice(LANES * p, LANES * (p + 1))
        vs = slice(2 * GLA_HEAD_V * p, 2 * GLA_HEAD_V * (p + 1))
        qp = q_t[:, ks]
        kp = k_t[:, ks]
        o_inter = []
        st = st_ref[p]
        for c in range(N_CHUNKS):
            rs = slice(CHUNK * c, CHUNK * (c + 1))
            o_inter.append(_dot_nt(qp[rs], st.astype(BF16)))
            ds = _dot_tn(v[rs, vs], k_end[rs, ks])
            st = st * dec[c][:, ks] + jnp.where(smask, ds, 0.0)
        st_ref[p] = st
        o_inter = jnp.concatenate(o_inter, axis=0)
        for hh in range(2):
            head = 2 * p + hh
            qm = jnp.where(lane_k == hh, qp, jnp.zeros_like(qp))
            sc = jnp.where(causal, _dot_nt(qm, kp), 0.0).astype(BF16)
            hv = slice(GLA_HEAD_V * head, GLA_HEAD_V * (head + 1))
            o_heads[head] = _dot(sc, v[:, hv]) + o_inter[:, GLA_HEAD_V * hh:GLA_HEAD_V * (hh + 1)]

    gnw = gnw_ref[...]
    y_heads = []
    for head in range(GLA_HEADS):
        o = o_heads[head]
        y_heads.append(o * lax.rsqrt(jnp.mean(o * o, axis=-1, keepdims=True) + EPS) * gnw)
    y_gla = jnp.concatenate(y_heads, axis=1) * (r * jax.nn.sigmoid(r))
    yield

    ycat = jnp.concatenate([y_pool, y_gla], axis=1).astype(BF16)
    h = x + _dot(ycat, wout_ref[...])
    yield

    xn = _rms(h, nfw_ref[...])
    xh = xn.astype(BF16)
    xl = (xn - xh.astype(F32)).astype(BF16)
    ph = _dot_nt(wr_ref[...], xh)
    plo = _dot_nt(wr_ref[...], xl)
    lt = ((ph[0:ROUTE_ROWS] + ph[ROUTE_ROWS:2 * ROUTE_ROWS])
          + (plo[0:ROUTE_ROWS] + plo[ROUTE_ROWS:2 * ROUTE_ROWS])) + br_ref[...]
    route = _route(lt)
    route_out[...] = route
    _store_records(rec_out, h, jnp.concatenate([route, jnp.zeros((LANES - 8, TN), F32)], axis=0).T)


def _mixer_kernel(x_ref, meta_ref, nmw_ref, winf_ref, wgu_ref, bg_ref, gnw_ref, pw_ref, ps_ref,
                  woutf_ref, nfw_ref, wr_ref, br_ref,
                  hext_ref, route_ref, win_ref, wout_ref, pvh_ref, st_ref):
    bi = pl.program_id(0)
    t = pl.program_id(1)

    @pl.when((bi == 0) & (t == 0))
    def _():
        win_ref[:, 0:C_G] = winf_ref[:, 0:C_G].astype(BF16)
        win_ref[:, C_G:IN_PAD] = jnp.zeros((D_MODEL, LANES), BF16)
        win_ref[:, C_G:C_G + GATE_RANK] = winf_ref[:, C_G:IN_COLS].astype(BF16)
        wout_ref[...] = woutf_ref[...].astype(BF16)

    @pl.when(t == 0)
    def _():
        smask = _state_mask()
        um = _rms(meta_ref[...], nmw_ref[...]).astype(BF16)
        pm = _dot(um, win_ref[...])
        la = _log_decay(pm[:, C_G:IN_PAD], wgu_ref, bg_ref)
        b = _seg_cumsum(la, N_META)
        k_end = (pm[:, C_K:C_V] * jnp.exp(b[N_META - 1:N_META, :] - b)).astype(BF16)
        vm = pm[:, C_V:C_R].astype(BF16)
        for p in range(N_PAIRS):
            ds = _dot_tn(vm[:, 2 * GLA_HEAD_V * p:2 * GLA_HEAD_V * (p + 1)],
                         k_end[:, LANES * p:LANES * (p + 1)])
            ds = jnp.where(smask, ds, 0.0)
            for s in range(SEQS):
                st_ref[s, p] = ds
        for s in range(SEQS):
            pvh_ref[s, 0:HIST, :] = pm[:, 0:POOL_WIDTH]

    tiles = [
        _mixer_tile(x_ref[s], pvh_ref.at[s], st_ref.at[s], nmw_ref, win_ref, wgu_ref, bg_ref, gnw_ref,
                    pw_ref, ps_ref, wout_ref, nfw_ref, wr_ref, br_ref,
                    hext_ref.at[s], route_ref.at[s])
        for s in range(SEQS)]
    _run_staggered(tiles, MIXER_LAG)


def _mixer_call(x, meta, nmw, winf, wgu, bg, gnw, pw, ps, woutf, nfw, wr, br):
    bsz, seq, _ = x.shape

    def full(shape, single=False):
        kw = dict(pipeline_mode=pl.Buffered(1)) if single else {}
        return pl.BlockSpec(shape, lambda b, t: (0,) * len(shape), **kw)

    return pl.pallas_call(
        _mixer_kernel,
        grid=(bsz // SEQS, seq // TN),
        in_specs=[
            pl.BlockSpec((SEQS, TN, D_MODEL), lambda b, t: (b, t, 0)),
            full((N_META, D_MODEL)),
            full((1, D_MODEL)),
            full((None, D_MODEL, IN_COLS), single=True),
            full((LANES, GLA_KDIM)),
            full((1, GLA_KDIM)),
            full((1, GLA_HEAD_V)),
            full((len(POOL_WINDOWS), POOL_GROUP, POOL_GROUP)),
            full((1, POOL_WIDTH)),
            full((None, D_MODEL, D_MODEL), single=True),
            full((1, D_MODEL)),
            full((2 * ROUTE_ROWS, D_MODEL)),
            full((ROUTE_ROWS, TN)),
        ],
        out_specs=[
            pl.BlockSpec((SEQS, TN * REC, LANES), lambda b, t: (b, t, 0)),
            pl.BlockSpec((SEQS, 8, TN), lambda b, t: (b, 0, t)),
        ],
        out_shape=[
            jax.ShapeDtypeStruct((bsz, seq * REC, LANES), F32),
            jax.ShapeDtypeStruct((bsz, 8, seq), F32),
        ],
        scratch_shapes=[
            pltpu.VMEM((D_MODEL, IN_PAD), BF16),
            pltpu.VMEM((D_MODEL, D_MODEL), BF16),
            pltpu.VMEM((SEQS, HIST + TN, POOL_WIDTH), F32),
            pltpu.VMEM((SEQS, N_PAIRS, 2 * GLA_HEAD_V, 2 * GLA_HEAD_K), F32),
        ],
        compiler_params=pltpu.CompilerParams(
            dimension_semantics=("arbitrary", "arbitrary"),
            vmem_limit_bytes=VMEM_LIMIT),
        name="mixer",
    )(x, meta, nmw, winf, wgu, bg, gnw, pw, ps, woutf, nfw, wr, br)


def _copy_records(slot_ref, tok_ref, slot_hbm, sem, to_slots):
    def body(gi, carry):
        for j in range(ISSUE_UNROLL):
            r = gi * ISSUE_UNROLL + j
            tok_rec = tok_ref.at[pl.ds(r * REC, REC), :]
            slot_rec = slot_hbm.at[pl.ds(slot_ref[0, r] * REC, REC), :]
            src, dst = (tok_rec, slot_rec) if to_slots else (slot_rec, tok_rec)
            pltpu.make_async_copy(src, dst, sem).start(priority=j % 2)
        return carry
    lax.fori_loop(0, slot_ref.shape[-1] // ISSUE_UNROLL, body, 0)
    pltpu.make_async_copy(tok_ref, tok_ref, sem).wait()


def _dispatch_kernel(zoff_ref, zflag_ref, slot_ref, tok_ref, rows_hbm, zeros_ref, sem, zsem):
    @pl.when(pl.program_id(0) == 0)
    def _():
        zeros_ref[...] = jnp.zeros(zeros_ref.shape, zeros_ref.dtype)
        for b in range(zflag_ref.shape[0]):
            @pl.when(zflag_ref[b] != 0)
            def _(b=b):
                off = pl.multiple_of(zoff_ref[b], TM * REC)
                pltpu.make_async_copy(zeros_ref, rows_hbm.at[pl.ds(off, TM * REC), :], zsem).start()
        for b in range(zflag_ref.shape[0]):
            @pl.when(zflag_ref[b] != 0)
            def _(b=b):
                off = pl.multiple_of(zoff_ref[b], TM * REC)
                pltpu.make_async_copy(zeros_ref, rows_hbm.at[pl.ds(off, TM * REC), :], zsem).wait()

    _copy_records(slot_ref, tok_ref, rows_hbm, sem, to_slots=True)


def _combine_kernel(slot_ref, rows_hbm, out_ref, stage_ref, sem):
    _copy_records(slot_ref, stage_ref, rows_hbm, sem, to_slots=False)
    for j in range(D_TILES):
        out_ref[:, LANES * j:LANES * (j + 1)] = stage_ref[pl.ds(j, PB_COMBINE, stride=REC), :]


def _slot_block_spec(index_map, pb):
    return pl.BlockSpec((None, 1, pb), index_map, memory_space=pltpu.SMEM)


def _dispatch_call(zoff, zflag, slots, recs, n_slots):
    n_tokens = recs.shape[0] // REC
    grid_spec = pltpu.PrefetchScalarGridSpec(
        num_scalar_prefetch=2,
        grid=(n_tokens // PB_DISPATCH,),
        in_specs=[
            _slot_block_spec(lambda i, zo, zf: (i, 0, 0), PB_DISPATCH),
            pl.BlockSpec((PB_DISPATCH * REC, LANES), lambda i, zo, zf: (i, 0)),
        ],
        out_specs=pl.BlockSpec(memory_space=pl.ANY),
        scratch_shapes=[
            pltpu.VMEM((TM * REC, LANES), F32),
            pltpu.SemaphoreType.DMA(()),
            pltpu.SemaphoreType.DMA(()),
        ],
    )
    return pl.pallas_call(
        _dispatch_kernel,
        grid_spec=grid_spec,
        out_shape=jax.ShapeDtypeStruct((n_slots * REC, LANES), F32),
        compiler_params=pltpu.CompilerParams(dimension_semantics=("arbitrary",),
                                             vmem_limit_bytes=VMEM_LIMIT),
        name="dispatch",
    )(zoff, zflag, slots, recs)


def _combine_call(slots, res, n_tokens):
    return pl.pallas_call(
        _combine_kernel,
        grid=(n_tokens // PB_COMBINE,),
        in_specs=[
            _slot_block_spec(lambda i: (i, 0, 0), PB_COMBINE),
            pl.BlockSpec(memory_space=pl.ANY),
        ],
        out_specs=pl.BlockSpec((PB_COMBINE, D_MODEL), lambda i: (i, 0)),
        out_shape=jax.ShapeDtypeStruct((n_tokens, D_MODEL), F32),
        scratch_shapes=[pltpu.VMEM((PB_COMBINE * REC, LANES), F32), pltpu.SemaphoreType.DMA(())],
        compiler_params=pltpu.CompilerParams(dimension_semantics=("arbitrary",),
                                             vmem_limit_bytes=VMEM_LIMIT),
        name="combine",
    )(slots, res)


def _moe_kernel(ea_ref, eb_ref, na_ref,
                rows_ref, wga_ref, wua_ref, wda_ref, wgb_ref, wub_ref, wdb_ref,
                nfw_ref, fnw_ref,
                out_ref, wgu_a, wd_a, wgu_b, wd_b):
    i = pl.program_id(0)
    n_act = na_ref[0]
    prev = jnp.maximum(i - 1, 0)

    @pl.when((i == 0) | (ea_ref[i] != ea_ref[prev]))
    def _():
        wgu_a[:, 0:EXPERT_HIDDEN] = wga_ref[...].astype(BF16)
        wgu_a[:, EXPERT_HIDDEN:2 * EXPERT_HIDDEN] = wua_ref[...].astype(BF16)
        wd_a[...] = wda_ref[...].astype(BF16)

    @pl.when((i == 0) | (eb_ref[i] != eb_ref[prev]))
    def _():
        wgu_b[:, 0:EXPERT_HIDDEN] = wgb_ref[...].astype(BF16)
        wgu_b[:, EXPERT_HIDDEN:2 * EXPERT_HIDDEN] = wub_ref[...].astype(BF16)
        wd_b[...] = wdb_ref[...].astype(BF16)

    def part(first, n):
        hrow, slab = _load_records(rows_ref, n, first)
        wa = slab[:, 3:4]
        wb = slab[:, 4:5]
        xn = _rms(hrow, nfw_ref[...]).astype(BF16)
        yield
        y = None
        for wgt, wgu, wd in ((wa, wgu_a, wd_a), (wb, wgu_b, wd_b)):
            gu = _dot(xn, wgu[...])
            yield
            gate = gu[:, 0:EXPERT_HIDDEN]
            hid = ((gate * jax.nn.sigmoid(gate)) * gu[:, EXPERT_HIDDEN:2 * EXPERT_HIDDEN]).astype(BF16)
            yield
            ye = wgt * _dot(hid, wd[...])
            y = ye if y is None else y + ye
            yield
        _store_records(out_ref, _rms(hrow + y, fnw_ref[...]), jnp.zeros((n, LANES), F32), first)

    @pl.when(i < n_act)
    def _():
        _run_staggered([part(k * (TM // MOE_PARTS), TM // MOE_PARTS) for k in range(MOE_PARTS)])

    @pl.when(i >= n_act)
    def _():
        out_ref[...] = jnp.zeros(out_ref.shape, out_ref.dtype)


def _moe_call(ea, eb, nact, rows, wg, wu, wd, nfw, fnw):
    n_tiles = rows.shape[0] // (TM * REC)
    sel_a = lambda i, ea, eb, na: (ea[i], 0, 0)
    sel_b = lambda i, ea, eb, na: (eb[i], 0, 0)
    tile = lambda i, ea, eb, na: (jnp.minimum(i, na[0] - 1), 0)
    up_spec = lambda sel: pl.BlockSpec((None, D_MODEL, EXPERT_HIDDEN), sel)
    down_spec = lambda sel: pl.BlockSpec((None, EXPERT_HIDDEN, D_MODEL), sel)
    grid_spec = pltpu.PrefetchScalarGridSpec(
        num_scalar_prefetch=3,
        grid=(n_tiles,),
        in_specs=[
            pl.BlockSpec((TM * REC, LANES), tile),
            up_spec(sel_a), up_spec(sel_a), down_spec(sel_a),
            up_spec(sel_b), up_spec(sel_b), down_spec(sel_b),
            pl.BlockSpec((1, D_MODEL), lambda i, ea, eb, na: (0, 0)),
            pl.BlockSpec((1, D_MODEL), lambda i, ea, eb, na: (0, 0)),
        ],
        out_specs=pl.BlockSpec((TM * REC, LANES), lambda i, ea, eb, na: (i, 0)),
        scratch_shapes=[
            pltpu.VMEM((D_MODEL, 2 * EXPERT_HIDDEN), BF16),
            pltpu.VMEM((EXPERT_HIDDEN, D_MODEL), BF16),
            pltpu.VMEM((D_MODEL, 2 * EXPERT_HIDDEN), BF16),
            pltpu.VMEM((EXPERT_HIDDEN, D_MODEL), BF16),
        ],
    )
    return pl.pallas_call(
        _moe_kernel,
        grid_spec=grid_spec,
        out_shape=jax.ShapeDtypeStruct((n_tiles * TM * REC, LANES), F32),
        compiler_params=pltpu.CompilerParams(
            dimension_semantics=("arbitrary",),
            vmem_limit_bytes=VMEM_LIMIT),
        name="moe",
    )(ea, eb, nact, rows, wg, wu, wd, wg, wu, wd, nfw, fnw)


def _slot_tables(route, n_tokens):
    g = route[:, 0, :].reshape(n_tokens).astype(I32)
    ja = route[:, 1, :].reshape(n_tokens).astype(I32)
    jb = route[:, 2, :].reshape(n_tokens).astype(I32)
    pair = (ja * (7 - ja)) // 2 + (jb - ja - 1)
    bucket = g * N_PAIR_BUCKETS + pair
    onehot = (bucket[None, :] == jnp.arange(N_BUCKETS, dtype=I32)[:, None]).astype(I32)
    csum = jnp.cumsum(onehot, axis=1)
    counts = csum[:, -1]
    rank = jnp.sum(csum * onehot, axis=0) - 1
    padded = ((counts + TM - 1) // TM) * TM
    ends = jnp.cumsum(padded)
    starts = ends - padded
    dst = jnp.take(starts, bucket) + rank
    n_tiles = n_tokens // TM + N_BUCKETS
    tile_start = jnp.arange(n_tiles, dtype=I32) * TM
    total = ends[-1]
    nact = total // TM
    tb = jnp.minimum(jnp.sum((tile_start[:, None] >= ends[None, :]).astype(I32), axis=1), N_BUCKETS - 1)
    tb = jnp.where(tile_start < total, tb, jnp.take(tb, jnp.maximum(nact - 1, 0)))
    pair_a = jnp.array([0, 0, 0, 1, 1, 2], I32)
    pair_b = jnp.array([1, 2, 3, 2, 3, 3], I32)
    ea = (tb // N_PAIR_BUCKETS) * EXPERTS_PER_GROUP + jnp.take(pair_a, tb % N_PAIR_BUCKETS)
    eb = (tb // N_PAIR_BUCKETS) * EXPERTS_PER_GROUP + jnp.take(pair_b, tb % N_PAIR_BUCKETS)
    tail = nact + jnp.arange(N_BUCKETS, dtype=I32)
    zflag = jnp.concatenate([(counts % TM != 0).astype(I32), (tail < n_tiles).astype(I32)])
    zoff = jnp.concatenate([jnp.maximum(ends - TM, 0), jnp.minimum(tail, n_tiles - 1) * TM]) * REC
    return ea, eb, jnp.reshape(nact, (1,)).astype(I32), zoff.astype(I32), zflag, dst


def kernel(x, meta_tokens, norm_mix_w, w_in, w_gate_up, b_gate, gla_norm_w, pool_w, pool_scale, w_out, norm_ffn_w, w_router_group, b_router_group, w_router_expert, b_router_expert, w_expert_gate, w_expert_up, w_expert_down, final_norm_w):
    bsz, seq, d = x.shape
    assert d == D_MODEL and seq % TN == 0 and bsz % SEQS == 0 and (bsz * seq) % TM == 0
    assert norm_mix_w.shape[0] == 1
    n_tokens = bsz * seq

    wgu_gate = jnp.pad(w_gate_up[0], ((0, LANES - GATE_RANK), (0, 0))).astype(BF16)
    w_r = jnp.pad(jnp.concatenate([w_router_group[0], w_router_expert[0]], axis=1).T,
                  ((0, ROUTE_ROWS - N_GROUPS - N_EXPERTS), (0, 0)))
    w_r_hi = w_r.astype(BF16)
    w_r_lo = (w_r - w_r_hi.astype(F32)).astype(BF16)
    b_r = jnp.pad(jnp.concatenate([b_router_group[0], b_router_expert[0]]),
                  (0, ROUTE_ROWS - N_GROUPS - N_EXPERTS))
    b_r = jnp.broadcast_to(b_r[:, None], (ROUTE_ROWS, TN))

    hext, route = _mixer_call(
        x, meta_tokens, norm_mix_w[0].reshape(1, d), w_in, wgu_gate, b_gate[0].reshape(1, GLA_KDIM),
        gla_norm_w[0].reshape(1, GLA_HEAD_V), pool_w[0].astype(BF16), pool_scale[0].reshape(1, POOL_WIDTH),
        w_out, norm_ffn_w[0].reshape(1, d), jnp.concatenate([w_r_hi, w_r_lo], axis=0), b_r)

    ea, eb, nact, zoff, zflag, slots = _slot_tables(route, n_tokens)
    n_slots = n_tokens + N_BUCKETS * TM
    rows = _dispatch_call(zoff, zflag, slots.reshape(n_tokens // PB_DISPATCH, 1, PB_DISPATCH),
                          hext.reshape(n_tokens * REC, LANES), n_slots)
    res = _moe_call(ea, eb, nact, rows, w_expert_gate[0], w_expert_up[0], w_expert_down[0],
                    norm_ffn_w[0].reshape(1, d), final_norm_w.reshape(1, d))
    out = _combine_call(slots.reshape(n_tokens // PB_COMBINE, 1, PB_COMBINE), res, n_tokens)
    return out.reshape(bsz, seq, d)
```

```python
import jax
import jax.numpy as jnp
from jax import lax
from jax.experimental import pallas as pl
from jax.experimental.pallas import tpu as pltpu

F32 = jnp.float32
BF16 = jnp.bfloat16
I32 = jnp.int32

D_MODEL = 1024
N_META = 16
POOL_WIDTH = 512
POOL_WINDOWS = (2, 4, 8, 16)
POOL_GROUP = 128
GLA_HEADS = 4
GLA_VDIM = 512
GLA_HEAD_V = 128
GLA_KDIM = 256
GLA_HEAD_K = 64
GATE_RANK = 16
GATE_NORMALIZER = 16.0
CHUNK = 64
N_GROUPS = 4
EXPERTS_PER_GROUP = 4
N_EXPERTS = 16
EXPERT_HIDDEN = 256
EPS = 1e-6

LANES = 128
SUBLANES = 8
IN_COLS = POOL_WIDTH + 2 * GLA_KDIM + 2 * GLA_VDIM + GATE_RANK
IN_PAD = POOL_WIDTH + 2 * GLA_KDIM + 2 * GLA_VDIM + LANES
C_Q = POOL_WIDTH
C_K = C_Q + GLA_KDIM
C_V = C_K + GLA_KDIM
C_R = C_V + GLA_VDIM
C_G = C_R + GLA_VDIM
D_TILES = D_MODEL // LANES
REC = D_TILES + 1
ROUTE_ROWS = 32

TN = 256
SEQS = 4
N_CHUNKS = TN // CHUNK
N_PAIRS = GLA_HEADS // 2
HIST = 16

MIXER_LAG = 1
TM = 512
MOE_PARTS = 1
PB_DISPATCH = 2048
PB_COMBINE = 2048
ISSUE_UNROLL = 8
N_PAIR_BUCKETS = 6
N_BUCKETS = N_GROUPS * N_PAIR_BUCKETS
VMEM_LIMIT = 56 * 1024 * 1024


def _dot(a, b):
    return jnp.dot(a, b, preferred_element_type=F32)


def _dot_nt(a, b):
    return lax.dot_general(a, b, (((1,), (1,)), ((), ())), preferred_element_type=F32)


def _dot_tn(a, b):
    return lax.dot_general(a, b, (((0,), (0,)), ((), ())), preferred_element_type=F32)


def _rms(x, w):
    return x * lax.rsqrt(jnp.mean(x * x, axis=-1, keepdims=True) + EPS) * w


def _log_sigmoid(z):
    return -(jnp.maximum(-z, 0.0) + jnp.log1p(jnp.exp(-jnp.abs(z))))


def _seg_cumsum(x, seg):
    pos = lax.broadcasted_iota(I32, x.shape, 0) % seg
    s = 1
    while s < seg:
        x = x + jnp.where(pos >= s, pltpu.roll(x, s, axis=0), 0.0)
        s *= 2
    return x


def _log_decay(g, wgu_ref, bg_ref):
    z = _dot(g.astype(BF16), wgu_ref[...]) + bg_ref[...]
    return _log_sigmoid(z) / GATE_NORMALIZER


def _state_mask():
    r = lax.broadcasted_iota(I32, (2 * GLA_HEAD_V, 2 * GLA_HEAD_K), 0) // GLA_HEAD_V
    c = lax.broadcasted_iota(I32, (2 * GLA_HEAD_V, 2 * GLA_HEAD_K), 1) // GLA_HEAD_K
    return r == c


def _first_index(vals, target):
    idx = jnp.full_like(target, float(len(vals) - 1))
    for j in range(len(vals) - 2, -1, -1):
        idx = jnp.where(vals[j] == target, float(j), idx)
    return idx


def _max_list(vals):
    m = vals[0]
    for val in vals[1:]:
        m = jnp.maximum(m, val)
    return m


def _route(lt):
    row = lambda i: lt[i:i + 1, :]
    gl = [row(i) for i in range(N_GROUPS)]
    gmax = _max_list(gl)
    gsel = _first_index(gl, gmax)
    gsum = gl[0] * 0.0
    for val in gl:
        gsum = gsum + jnp.exp(val - gmax)
    p_g = 1.0 / gsum
    el = []
    for j in range(EXPERTS_PER_GROUP):
        val = row(N_GROUPS + EXPERTS_PER_GROUP * (N_GROUPS - 1) + j)
        for g in range(N_GROUPS - 2, -1, -1):
            val = jnp.where(gsel == float(g), row(N_GROUPS + EXPERTS_PER_GROUP * g + j), val)
        el.append(val)
    emax = _max_list(el)
    ee = [jnp.exp(val - emax) for val in el]
    esum = ee[0]
    for val in ee[1:]:
        esum = esum + val
    ep = [val / esum for val in ee]
    p1 = _max_list(ep)
    j1 = _first_index(ep, p1)
    ep_rest = [jnp.where(j1 == float(j), -1.0, ep[j]) for j in range(EXPERTS_PER_GROUP)]
    p2 = _max_list(ep_rest)
    j2 = _first_index(ep_rest, p2)
    den = p1 + p2
    w1 = p_g * (p1 / den)
    w2 = p_g * (p2 / den)
    first_low = j1 < j2
    ja = jnp.where(first_low, j1, j2)
    jb = jnp.where(first_low, j2, j1)
    wa = jnp.where(first_low, w1, w2)
    wb = jnp.where(first_low, w2, w1)
    zero = jnp.zeros_like(wa)
    return jnp.concatenate([gsel, ja, jb, wa, wb, zero, zero, zero], axis=0)


def _store_records(rec_ref, rows, last_row, first=0):
    n = rows.shape[0]
    for j in range(D_TILES):
        rec_ref[pl.ds(first * REC + j, n, stride=REC), :] = rows[:, LANES * j:LANES * (j + 1)]
    rec_ref[pl.ds(first * REC + D_TILES, n, stride=REC), :] = last_row


def _load_records(rec_ref, n, first=0):
    rows = jnp.concatenate(
        [rec_ref[pl.ds(first * REC + j, n, stride=REC), :] for j in range(D_TILES)], axis=1)
    return rows, rec_ref[pl.ds(first * REC + D_TILES, n, stride=REC), :]


def _run_staggered(parts, lag=1):
    live = list(range(len(parts)))
    step = 0
    while live:
        for k in list(live):
            if step >= k * lag and next(parts[k], "done") == "done":
                live.remove(k)
        step += 1


def _mixer_tile(x, pvh_ref, st_ref, nmw_ref, win_ref, wgu_ref, bg_ref, gnw_ref, pw_ref, ps_ref,
                wout_ref, nfw_ref, wr_ref, br_ref, rec_out, route_out):
    smask = _state_mask()
    u = _rms(x, nmw_ref[...]).astype(BF16)
    proj = _dot(u, win_ref[...])
    yield

    pv = proj[:, 0:POOL_WIDTH]
    pvh_ref[HIST:HIST + TN, :] = pv
    ext = pvh_ref[...]
    sums = [ext[:, POOL_GROUP * gi:POOL_GROUP * (gi + 1)] for gi in range(len(POOL_WINDOWS))]
    shift = 1
    for stage in range(len(POOL_WINDOWS)):
        for gi in range(stage, len(POOL_WINDOWS)):
            sums[gi] = sums[gi] + pltpu.roll(sums[gi], shift, axis=0)
        shift *= 2
    mixed = []
    for gi, w in enumerate(POOL_WINDOWS):
        cs = slice(POOL_GROUP * gi, POOL_GROUP * (gi + 1))
        pooled = sums[gi][HIST:HIST + TN, :] * (1.0 / w) - pv[:, cs]
        mixed.append(_dot(pooled.astype(BF16), pw_ref[gi]))
    y_pool = jnp.concatenate(mixed, axis=1) * ps_ref[...]
    pvh_ref[0:HIST, :] = ext[TN:TN + HIST, :]

    q = proj[:, C_Q:C_K]
    k = proj[:, C_K:C_V]
    v = proj[:, C_V:C_R].astype(BF16)
    r = proj[:, C_R:C_G]
    la = _log_decay(proj[:, C_G:IN_PAD], wgu_ref, bg_ref)
    b = _seg_cumsum(la, CHUNK)
    b3 = b.reshape(N_CHUNKS, CHUNK, GLA_KDIM)
    bl = b3[:, CHUNK - 1:CHUNK, :]
    blb = jnp.broadcast_to(bl, (N_CHUNKS, CHUNK, GLA_KDIM)).reshape(TN, GLA_KDIM)
    q_t = ((q * (GLA_HEAD_K ** -0.5)) * jnp.exp(b)).astype(BF16)
    k_t = (k * jnp.exp(-b)).astype(BF16)
    k_end = (k * jnp.exp(blb - b)).astype(BF16)
    dec = jnp.exp(bl)
    yield

    ri = lax.broadcasted_iota(I32, (TN, TN), 0)
    ci = lax.broadcasted_iota(I32, (TN, TN), 1)
    causal = (ri // CHUNK == ci // CHUNK) & (ci <= ri)
    lane_k = lax.broadcasted_iota(I32, (TN, LANES), 1) // GLA_HEAD_K

    o_heads = [None] * GLA_HEADS
    for p in range(N_PAIRS):
        ks = slice(LANES * p, LANES * (p + 1))
        vs = slice(2 * GLA_HEAD_V * p, 2 * GLA_HEAD_V * (p + 1))
        qp = q_t[:, ks]
        kp = k_t[:, ks]
        o_inter = []
        st = st_ref[p]
        for c in range(N_CHUNKS):
            rs = slice(CHUNK * c, CHUNK * (c + 1))
            o_inter.append(_dot_nt(qp[rs], st.astype(BF16)))
            ds = _dot_tn(v[rs, vs], k_end[rs, ks])
            st = st * dec[c][:, ks] + jnp.where(smask, ds, 0.0)
        st_ref[p] = st
        o_inter = jnp.concatenate(o_inter, axis=0)
        for hh in range(2):
            head = 2 * p + hh
            qm = jnp.where(lane_k == hh, qp, jnp.zeros_like(qp))
            sc = jnp.where(causal, _dot_nt(qm, kp), 0.0).astype(BF16)
            hv = slice(GLA_HEAD_V * head, GLA_HEAD_V * (head + 1))
            o_heads[head] = _dot(sc, v[:, hv]) + o_inter[:, GLA_HEAD_V * hh:GLA_HEAD_V * (hh + 1)]

    gnw = gnw_ref[...]
    y_heads = []
    for head in range(GLA_HEADS):
        o = o_heads[head]
        y_heads.append(o * lax.rsqrt(jnp.mean(o * o, axis=-1, keepdims=True) + EPS) * gnw)
    y_gla = jnp.concatenate(y_heads, axis=1) * (r * jax.nn.sigmoid(r))
    yield

    ycat = jnp.concatenate([y_pool, y_gla], axis=1).astype(BF16)
    h = x + _dot(ycat, wout_ref[...])
    yield

    xn = _rms(h, nfw_ref[...])
    xh = xn.astype(BF16)
    xl = (xn - xh.astype(F32)).astype(BF16)
    ph = _dot_nt(wr_ref[...], xh)
    plo = _dot_nt(wr_ref[...], xl)
    lt = ((ph[0:ROUTE_ROWS] + ph[ROUTE_ROWS:2 * ROUTE_ROWS])
          + (plo[0:ROUTE_ROWS] + plo[ROUTE_ROWS:2 * ROUTE_ROWS])) + br_ref[...]
    route = _route(lt)
    route_out[...] = route
    _store_records(rec_out, h, jnp.concatenate([route, jnp.zeros((LANES - 8, TN), F32)], axis=0).T)


def _mixer_kernel(x_ref, meta_ref, nmw_ref, winf_ref, wgu_ref, bg_ref, gnw_ref, pw_ref, ps_ref,
                  woutf_ref, nfw_ref, wr_ref, br_ref,
                  hext_ref, route_ref, win_ref, wout_ref, pvh_ref, st_ref):
    bi = pl.program_id(0)
    t = pl.program_id(1)

    @pl.when((bi == 0) & (t == 0))
    def _():
        win_ref[:, 0:C_G] = winf_ref[:, 0:C_G].astype(BF16)
        win_ref[:, C_G:IN_PAD] = jnp.zeros((D_MODEL, LANES), BF16)
        win_ref[:, C_G:C_G + GATE_RANK] = winf_ref[:, C_G:IN_COLS].astype(BF16)
        wout_ref[...] = woutf_ref[...].astype(BF16)

    @pl.when(t == 0)
    def _():
        smask = _state_mask()
        um = _rms(meta_ref[...], nmw_ref[...]).astype(BF16)
        pm = _dot(um, win_ref[...])
        la = _log_decay(pm[:, C_G:IN_PAD], wgu_ref, bg_ref)
        b = _seg_cumsum(la, N_META)
        k_end = (pm[:, C_K:C_V] * jnp.exp(b[N_META - 1:N_META, :] - b)).astype(BF16)
        vm = pm[:, C_V:C_R].astype(BF16)
        for p in range(N_PAIRS):
            ds = _dot_tn(vm[:, 2 * GLA_HEAD_V * p:2 * GLA_HEAD_V * (p + 1)],
                         k_end[:, LANES * p:LANES * (p + 1)])
            ds = jnp.where(smask, ds, 0.0)
            for s in range(SEQS):
                st_ref[s, p] = ds
        for s in range(SEQS):
            pvh_ref[s, 0:HIST, :] = pm[:, 0:POOL_WIDTH]

    tiles = [
        _mixer_tile(x_ref[s], pvh_ref.at[s], st_ref.at[s], nmw_ref, win_ref, wgu_ref, bg_ref, gnw_ref,
                    pw_ref, ps_ref, wout_ref, nfw_ref, wr_ref, br_ref,
                    hext_ref.at[s], route_ref.at[s])
        for s in range(SEQS)]
    _run_staggered(tiles, MIXER_LAG)


def _mixer_call(x, meta, nmw, winf, wgu, bg, gnw, pw, ps, woutf, nfw, wr, br):
    bsz, seq, _ = x.shape

    def full(shape, single=False):
        kw = dict(pipeline_mode=pl.Buffered(1)) if single else {}
        return pl.BlockSpec(shape, lambda b, t: (0,) * len(shape), **kw)

    return pl.pallas_call(
        _mixer_kernel,
        grid=(bsz // SEQS, seq // TN),
        in_specs=[
            pl.BlockSpec((SEQS, TN, D_MODEL), lambda b, t: (b, t, 0)),
            full((N_META, D_MODEL)),
            full((1, D_MODEL)),
            full((None, D_MODEL, IN_COLS), single=True),
            full((LANES, GLA_KDIM)),
            full((1, GLA_KDIM)),
            full((1, GLA_HEAD_V)),
            full((len(POOL_WINDOWS), POOL_GROUP, POOL_GROUP)),
            full((1, POOL_WIDTH)),
            full((None, D_MODEL, D_MODEL), single=True),
            full((1, D_MODEL)),
            full((2 * ROUTE_ROWS, D_MODEL)),
            full((ROUTE_ROWS, TN)),
        ],
        out_specs=[
            pl.BlockSpec((SEQS, TN * REC, LANES), lambda b, t: (b, t, 0)),
            pl.BlockSpec((SEQS, 8, TN), lambda b, t: (b, 0, t)),
        ],
        out_shape=[
            jax.ShapeDtypeStruct((bsz, seq * REC, LANES), F32),
            jax.ShapeDtypeStruct((bsz, 8, seq), F32),
        ],
        scratch_shapes=[
            pltpu.VMEM((D_MODEL, IN_PAD), BF16),
            pltpu.VMEM((D_MODEL, D_MODEL), BF16),
            pltpu.VMEM((SEQS, HIST + TN, POOL_WIDTH), F32),
            pltpu.VMEM((SEQS, N_PAIRS, 2 * GLA_HEAD_V, 2 * GLA_HEAD_K), F32),
        ],
        compiler_params=pltpu.CompilerParams(
            dimension_semantics=("arbitrary", "arbitrary"),
            vmem_limit_bytes=VMEM_LIMIT),
        name="mixer",
    )(x, meta, nmw, winf, wgu, bg, gnw, pw, ps, woutf, nfw, wr, br)


def _copy_records(slot_ref, tok_ref, slot_hbm, sem, to_slots):
    def body(gi, carry):
        for j in range(ISSUE_UNROLL):
            r = gi * ISSUE_UNROLL + j
            tok_rec = tok_ref.at[pl.ds(r * REC, REC), :]
            slot_rec = slot_hbm.at[pl.ds(slot_ref[0, r] * REC, REC), :]
            src, dst = (tok_rec, slot_rec) if to_slots else (slot_rec, tok_rec)
            pltpu.make_async_copy(src, dst, sem).start(priority=j % 2)
        return carry
    lax.fori_loop(0, slot_ref.shape[-1] // ISSUE_UNROLL, body, 0)
    pltpu.make_async_copy(tok_ref, tok_ref, sem).wait()


def _dispatch_kernel(zoff_ref, zflag_ref, slot_ref, tok_ref, rows_hbm, zeros_ref, sem, zsem):
    @pl.when(pl.program_id(0) == 0)
    def _():
        zeros_ref[...] = jnp.zeros(zeros_ref.shape, zeros_ref.dtype)
        for b in range(zflag_ref.shape[0]):
            @pl.when(zflag_ref[b] != 0)
            def _(b=b):
                off = pl.multiple_of(zoff_ref[b], TM * REC)
                pltpu.make_async_copy(zeros_ref, rows_hbm.at[pl.ds(off, TM * REC), :], zsem).start()
        for b in range(zflag_ref.shape[0]):
            @pl.when(zflag_ref[b] != 0)
            def _(b=b):
                off = pl.multiple_of(zoff_ref[b], TM * REC)
                pltpu.make_async_copy(zeros_ref, rows_hbm.at[pl.ds(off, TM * REC), :], zsem).wait()

    _copy_records(slot_ref, tok_ref, rows_hbm, sem, to_slots=True)


def _combine_kernel(slot_ref, rows_hbm, out_ref, stage_ref, sem):
    _copy_records(slot_ref, stage_ref, rows_hbm, sem, to_slots=False)
    for j in range(D_TILES):
        out_ref[:, LANES * j:LANES * (j + 1)] = stage_ref[pl.ds(j, PB_COMBINE, stride=REC), :]


def _slot_block_spec(index_map, pb):
    return pl.BlockSpec((None, 1, pb), index_map, memory_space=pltpu.SMEM)


def _dispatch_call(zoff, zflag, slots, recs, n_slots):
    n_tokens = recs.shape[0] // REC
    grid_spec = pltpu.PrefetchScalarGridSpec(
        num_scalar_prefetch=2,
        grid=(n_tokens // PB_DISPATCH,),
        in_specs=[
            _slot_block_spec(lambda i, zo, zf: (i, 0, 0), PB_DISPATCH),
            pl.BlockSpec((PB_DISPATCH * REC, LANES), lambda i, zo, zf: (i, 0)),
        ],
        out_specs=pl.BlockSpec(memory_space=pl.ANY),
        scratch_shapes=[
            pltpu.VMEM((TM * REC, LANES), F32),
            pltpu.SemaphoreType.DMA(()),
            pltpu.SemaphoreType.DMA(()),
        ],
    )
    return pl.pallas_call(
        _dispatch_kernel,
        grid_spec=grid_spec,
        out_shape=jax.ShapeDtypeStruct((n_slots * REC, LANES), F32),
        compiler_params=pltpu.CompilerParams(dimension_semantics=("arbitrary",),
                                             vmem_limit_bytes=VMEM_LIMIT),
        name="dispatch",
    )(zoff, zflag, slots, recs)


def _combine_call(slots, res, n_tokens):
    return pl.pallas_call(
        _combine_kernel,
        grid=(n_tokens // PB_COMBINE,),
        in_specs=[
            _slot_block_spec(lambda i: (i, 0, 0), PB_COMBINE),
            pl.BlockSpec(memory_space=pl.ANY),
        ],
        out_specs=pl.BlockSpec((PB_COMBINE, D_MODEL), lambda i: (i, 0)),
        out_shape=jax.ShapeDtypeStruct((n_tokens, D_MODEL), F32),
        scratch_shapes=[pltpu.VMEM((PB_COMBINE * REC, LANES), F32), pltpu.SemaphoreType.DMA(())],
        compiler_params=pltpu.CompilerParams(dimension_semantics=("arbitrary",),
                                             vmem_limit_bytes=VMEM_LIMIT),
        name="combine",
    )(slots, res)


def _moe_kernel(ea_ref, eb_ref, na_ref,
                rows_ref, wga_ref, wua_ref, wda_ref, wgb_ref, wub_ref, wdb_ref,
                nfw_ref, fnw_ref,
                out_ref, wgu_a, wd_a, wgu_b, wd_b):
    i = pl.program_id(0)
    n_act = na_ref[0]
    prev = jnp.maximum(i - 1, 0)

    @pl.when((i == 0) | (ea_ref[i] != ea_ref[prev]))
    def _():
        wgu_a[:, 0:EXPERT_HIDDEN] = wga_ref[...].astype(BF16)
        wgu_a[:, EXPERT_HIDDEN:2 * EXPERT_HIDDEN] = wua_ref[...].astype(BF16)
        wd_a[...] = wda_ref[...].astype(BF16)

    @pl.when((i == 0) | (eb_ref[i] != eb_ref[prev]))
    def _():
        wgu_b[:, 0:EXPERT_HIDDEN] = wgb_ref[...].astype(BF16)
        wgu_b[:, EXPERT_HIDDEN:2 * EXPERT_HIDDEN] = wub_ref[...].astype(BF16)
        wd_b[...] = wdb_ref[...].astype(BF16)

    def part(first, n):
        hrow, slab = _load_records(rows_ref, n, first)
        wa = slab[:, 3:4]
        wb = slab[:, 4:5]
        xn = _rms(hrow, nfw_ref[...]).astype(BF16)
        yield
        y = None
        for wgt, wgu, wd in ((wa, wgu_a, wd_a), (wb, wgu_b, wd_b)):
            gu = _dot(xn, wgu[...])
            yield
            gate = gu[:, 0:EXPERT_HIDDEN]
            hid = ((gate * jax.nn.sigmoid(gate)) * gu[:, EXPERT_HIDDEN:2 * EXPERT_HIDDEN]).astype(BF16)
            yield
            ye = wgt * _dot(hid, wd[...])
            y = ye if y is None else y + ye
            yield
        _store_records(out_ref, _rms(hrow + y, fnw_ref[...]), jnp.zeros((n, LANES), F32), first)

    @pl.when(i < n_act)
    def _():
        _run_staggered([part(k * (TM // MOE_PARTS), TM // MOE_PARTS) for k in range(MOE_PARTS)])

    @pl.when(i >= n_act)
    def _():
        out_ref[...] = jnp.zeros(out_ref.shape, out_ref.dtype)


def _moe_call(ea, eb, nact, rows, wg, wu, wd, nfw, fnw):
    n_tiles = rows.shape[0] // (TM * REC)
    sel_a = lambda i, ea, eb, na: (ea[i], 0, 0)
    sel_b = lambda i, ea, eb, na: (eb[i], 0, 0)
    tile = lambda i, ea, eb, na: (jnp.minimum(i, na[0] - 1), 0)
    up_spec = lambda sel: pl.BlockSpec((None, D_MODEL, EXPERT_HIDDEN), sel)
    down_spec = lambda sel: pl.BlockSpec((None, EXPERT_HIDDEN, D_MODEL), sel)
    grid_spec = pltpu.PrefetchScalarGridSpec(
        num_scalar_prefetch=3,
        grid=(n_tiles,),
        in_specs=[
            pl.BlockSpec((TM * REC, LANES), tile),
            up_spec(sel_a), up_spec(sel_a), down_spec(sel_a),
            up_spec(sel_b), up_spec(sel_b), down_spec(sel_b),
            pl.BlockSpec((1, D_MODEL), lambda i, ea, eb, na: (0, 0)),
            pl.BlockSpec((1, D_MODEL), lambda i, ea, eb, na: (0, 0)),
        ],
        out_specs=pl.BlockSpec((TM * REC, LANES), lambda i, ea, eb, na: (i, 0)),
        scratch_shapes=[
            pltpu.VMEM((D_MODEL, 2 * EXPERT_HIDDEN), BF16),
            pltpu.VMEM((EXPERT_HIDDEN, D_MODEL), BF16),
            pltpu.VMEM((D_MODEL, 2 * EXPERT_HIDDEN), BF16),
            pltpu.VMEM((EXPERT_HIDDEN, D_MODEL), BF16),
        ],
    )
    return pl.pallas_call(
        _moe_kernel,
        grid_spec=grid_spec,
        out_shape=jax.ShapeDtypeStruct((n_tiles * TM * REC, LANES), F32),
        compiler_params=pltpu.CompilerParams(
            dimension_semantics=("arbitrary",),
            vmem_limit_bytes=VMEM_LIMIT),
        name="moe",
    )(ea, eb, nact, rows, wg, wu, wd, wg, wu, wd, nfw, fnw)


def _slot_tables(route, n_tokens):
    g = route[:, 0, :].reshape(n_tokens).astype(I32)
    ja = route[:, 1, :].reshape(n_tokens).astype(I32)
    jb = route[:, 2, :].reshape(n_tokens).astype(I32)
    pair = (ja * (7 - ja)) // 2 + (jb - ja - 1)
    bucket = g * N_PAIR_BUCKETS + pair
    onehot = (bucket[None, :] == jnp.arange(N_BUCKETS, dtype=I32)[:, None]).astype(I32)
    csum = jnp.cumsum(onehot, axis=1)
    counts = csum[:, -1]
    rank = jnp.sum(csum * onehot, axis=0) - 1
    padded = ((counts + TM - 1) // TM) * TM
    ends = jnp.cumsum(padded)
    starts = ends - padded
    dst = jnp.sum(onehot * starts[:, None], axis=0) + rank
    n_tiles = n_tokens // TM + N_BUCKETS
    tile_start = jnp.arange(n_tiles, dtype=I32) * TM
    total = ends[-1]
    nact = total // TM
    tb = jnp.minimum(jnp.sum((tile_start[:, None] >= ends[None, :]).astype(I32), axis=1), N_BUCKETS - 1)
    tb = jnp.where(tile_start < total, tb, jnp.take(tb, jnp.maximum(nact - 1, 0)))
    pair_a = jnp.array([0, 0, 0, 1, 1, 2], I32)
    pair_b = jnp.array([1, 2, 3, 2, 3, 3], I32)
    ea = (tb // N_PAIR_BUCKETS) * EXPERTS_PER_GROUP + jnp.take(pair_a, tb % N_PAIR_BUCKETS)
    eb = (tb // N_PAIR_BUCKETS) * EXPERTS_PER_GROUP + jnp.take(pair_b, tb % N_PAIR_BUCKETS)
    tail = nact + jnp.arange(N_BUCKETS, dtype=I32)
    zflag = jnp.concatenate([(counts % TM != 0).astype(I32), (tail < n_tiles).astype(I32)])
    zoff = jnp.concatenate([jnp.maximum(ends - TM, 0), jnp.minimum(tail, n_tiles - 1) * TM]) * REC
    return ea, eb, jnp.reshape(nact, (1,)).astype(I32), zoff.astype(I32), zflag, dst


def kernel(x, meta_tokens, norm_mix_w, w_in, w_gate_up, b_gate, gla_norm_w, pool_w, pool_scale, w_out, norm_ffn_w, w_router_group, b_router_group, w_router_expert, b_router_expert, w_expert_gate, w_expert_up, w_expert_down, final_norm_w):
    bsz, seq, d = x.shape
    assert d == D_MODEL and seq % TN == 0 and bsz % SEQS == 0 and (bsz * seq) % TM == 0
    assert norm_mix_w.shape[0] == 1
    n_tokens = bsz * seq

    wgu_gate = jnp.pad(w_gate_up[0], ((0, LANES - GATE_RANK), (0, 0))).astype(BF16)
    w_r = jnp.pad(jnp.concatenate([w_router_group[0], w_router_expert[0]], axis=1).T,
                  ((0, ROUTE_ROWS - N_GROUPS - N_EXPERTS), (0, 0)))
    w_r_hi = w_r.astype(BF16)
    w_r_lo = (w_r - w_r_hi.astype(F32)).astype(BF16)
    b_r = jnp.pad(jnp.concatenate([b_router_group[0], b_router_expert[0]]),
                  (0, ROUTE_ROWS - N_GROUPS - N_EXPERTS))
    b_r = jnp.broadcast_to(b_r[:, None], (ROUTE_ROWS, TN))

    hext, route = _mixer_call(
        x, meta_tokens, norm_mix_w[0].reshape(1, d), w_in, wgu_gate, b_gate[0].reshape(1, GLA_KDIM),
        gla_norm_w[0].reshape(1, GLA_HEAD_V), pool_w[0].astype(BF16), pool_scale[0].reshape(1, POOL_WIDTH),
        w_out, norm_ffn_w[0].reshape(1, d), jnp.concatenate([w_r_hi, w_r_lo], axis=0), b_r)

    ea, eb, nact, zoff, zflag, slots = _slot_tables(route, n_tokens)
    n_slots = n_tokens + N_BUCKETS * TM
    rows = _dispatch_call(zoff, zflag, slots.reshape(n_tokens // PB_DISPATCH, 1, PB_DISPATCH),
                          hext.reshape(n_tokens * REC, LANES), n_slots)
    res = _moe_call(ea, eb, nact, rows, w_expert_gate[0], w_expert_up[0], w_expert_down[0],
                    norm_ffn_w[0].reshape(1, d), final_norm_w.reshape(1, d))
    out = _combine_call(slots.reshape(n_tokens // PB_COMBINE, 1, PB_COMBINE), res, n_tokens)
    return out.reshape(bsz, seq, d)
```

```python
import jax
import jax.numpy as jnp
from jax import lax
from jax.experimental import pallas as pl
from jax.experimental.pallas import tpu as pltpu

F32 = jnp.float32
BF16 = jnp.bfloat16
I32 = jnp.int32

D_MODEL = 1024
N_META = 16
POOL_WIDTH = 512
POOL_WINDOWS = (2, 4, 8, 16)
POOL_GROUP = 128
GLA_HEADS = 4
GLA_VDIM = 512
GLA_HEAD_V = 128
GLA_KDIM = 256
GLA_HEAD_K = 64
GATE_RANK = 16
GATE_NORMALIZER = 16.0
CHUNK = 64
N_GROUPS = 4
EXPERTS_PER_GROUP = 4
N_EXPERTS = 16
EXPERT_HIDDEN = 256
EPS = 1e-6

LANES = 128
SUBLANES = 8
IN_COLS = POOL_WIDTH + 2 * GLA_KDIM + 2 * GLA_VDIM + GATE_RANK
IN_PAD = POOL_WIDTH + 2 * GLA_KDIM + 2 * GLA_VDIM + LANES
C_Q = POOL_WIDTH
C_K = C_Q + GLA_KDIM
C_V = C_K + GLA_KDIM
C_R = C_V + GLA_VDIM
C_G = C_R + GLA_VDIM
D_TILES = D_MODEL // LANES
REC = D_TILES + 1
ROUTE_ROWS = 32

TN = 256
SEQS = 4
N_CHUNKS = TN // CHUNK
N_PAIRS = GLA_HEADS // 2
HIST = 16

MIXER_LAG = 1
TM = 512
MOE_PARTS = 1
PB_DISPATCH = 2048
PB_COMBINE = 2048
ISSUE_UNROLL = 8
N_PAIR_BUCKETS = 6
N_BUCKETS = N_GROUPS * N_PAIR_BUCKETS
VMEM_LIMIT = 56 * 1024 * 1024


def _dot(a, b):
    return jnp.dot(a, b, preferred_element_type=F32)


def _dot_nt(a, b):
    return lax.dot_general(a, b, (((1,), (1,)), ((), ())), preferred_element_type=F32)


def _dot_tn(a, b):
    return lax.dot_general(a, b, (((0,), (0,)), ((), ())), preferred_element_type=F32)


def _rms(x, w):
    return x * lax.rsqrt(jnp.mean(x * x, axis=-1, keepdims=True) + EPS) * w


def _log_sigmoid(z):
    return -(jnp.maximum(-z, 0.0) + jnp.log(1.0 + jnp.exp(-jnp.abs(z))))


def _seg_cumsum(x, seg):
    pos = lax.broadcasted_iota(I32, x.shape, 0) % seg
    s = 1
    while s < seg:
        x = x + jnp.where(pos >= s, pltpu.roll(x, s, axis=0), 0.0)
        s *= 2
    return x


def _log_decay(g, wgu_ref, bg_ref):
    z = _dot(g.astype(BF16), wgu_ref[...]) + bg_ref[...]
    return _log_sigmoid(z) / GATE_NORMALIZER


def _state_mask():
    r = lax.broadcasted_iota(I32, (2 * GLA_HEAD_V, 2 * GLA_HEAD_K), 0) // GLA_HEAD_V
    c = lax.broadcasted_iota(I32, (2 * GLA_HEAD_V, 2 * GLA_HEAD_K), 1) // GLA_HEAD_K
    return r == c


def _first_index(vals, target):
    idx = jnp.full_like(target, float(len(vals) - 1))
    for j in range(len(vals) - 2, -1, -1):
        idx = jnp.where(vals[j] == target, float(j), idx)
    return idx


def _max_list(vals):
    m = vals[0]
    for val in vals[1:]:
        m = jnp.maximum(m, val)
    return m


def _route(lt):
    row = lambda i: lt[i:i + 1, :]
    gl = [row(i) for i in range(N_GROUPS)]
    gmax = _max_list(gl)
    gsel = _first_index(gl, gmax)
    gsum = gl[0] * 0.0
    for val in gl:
        gsum = gsum + jnp.exp(val - gmax)
    p_g = 1.0 / gsum
    el = []
    for j in range(EXPERTS_PER_GROUP):
        val = row(N_GROUPS + EXPERTS_PER_GROUP * (N_GROUPS - 1) + j)
        for g in range(N_GROUPS - 2, -1, -1):
            val = jnp.where(gsel == float(g), row(N_GROUPS + EXPERTS_PER_GROUP * g + j), val)
        el.append(val)
    emax = _max_list(el)
    ee = [jnp.exp(val - emax) for val in el]
    esum = ee[0]
    for val in ee[1:]:
        esum = esum + val
    ep = [val / esum for val in ee]
    p1 = _max_list(ep)
    j1 = _first_index(ep, p1)
    ep_rest = [jnp.where(j1 == float(j), -1.0, ep[j]) for j in range(EXPERTS_PER_GROUP)]
    p2 = _max_list(ep_rest)
    j2 = _first_index(ep_rest, p2)
    den = p1 + p2
    w1 = p_g * (p1 / den)
    w2 = p_g * (p2 / den)
    first_low = j1 < j2
    ja = jnp.where(first_low, j1, j2)
    jb = jnp.where(first_low, j2, j1)
    wa = jnp.where(first_low, w1, w2)
    wb = jnp.where(first_low, w2, w1)
    zero = jnp.zeros_like(wa)
    return jnp.concatenate([gsel, ja, jb, wa, wb, zero, zero, zero], axis=0)


def _store_records(rec_ref, rows, last_row, first=0):
    n = rows.shape[0]
    for j in range(D_TILES):
        rec_ref[pl.ds(first * REC + j, n, stride=REC), :] = rows[:, LANES * j:LANES * (j + 1)]
    rec_ref[pl.ds(first * REC + D_TILES, n, stride=REC), :] = last_row


def _load_records(rec_ref, n, first=0):
    rows = jnp.concatenate(
        [rec_ref[pl.ds(first * REC + j, n, stride=REC), :] for j in range(D_TILES)], axis=1)
    return rows, rec_ref[pl.ds(first * REC + D_TILES, n, stride=REC), :]


def _run_staggered(parts, lag=1):
    live = list(range(len(parts)))
    step = 0
    while live:
        for k in list(live):
            if step >= k * lag and next(parts[k], "done") == "done":
                live.remove(k)
        step += 1


def _mixer_tile(x, pvh_ref, st_ref, nmw_ref, win_ref, wgu_ref, bg_ref, gnw_ref, pw_ref, ps_ref,
                wout_ref, nfw_ref, wr_ref, br_ref, rec_out, route_out):
    smask = _state_mask()
    u = _rms(x, nmw_ref[...]).astype(BF16)
    proj = _dot(u, win_ref[...])
    yield

    pv = proj[:, 0:POOL_WIDTH]
    pvh_ref[HIST:HIST + TN, :] = pv
    ext = pvh_ref[...]
    sums = [ext[:, POOL_GROUP * gi:POOL_GROUP * (gi + 1)] for gi in range(len(POOL_WINDOWS))]
    shift = 1
    for stage in range(len(POOL_WINDOWS)):
        for gi in range(stage, len(POOL_WINDOWS)):
            sums[gi] = sums[gi] + pltpu.roll(sums[gi], shift, axis=0)
        shift *= 2
    mixed = []
    for gi, w in enumerate(POOL_WINDOWS):
        cs = slice(POOL_GROUP * gi, POOL_GROUP * (gi + 1))
        pooled = sums[gi][HIST:HIST + TN, :] * (1.0 / w) - pv[:, cs]
        mixed.append(_dot(pooled.astype(BF16), pw_ref[gi]))
    y_pool = jnp.concatenate(mixed, axis=1) * ps_ref[...]
    pvh_ref[0:HIST, :] = ext[TN:TN + HIST, :]

    q = proj[:, C_Q:C_K]
    k = proj[:, C_K:C_V]
    v = proj[:, C_V:C_R].astype(BF16)
    r = proj[:, C_R:C_G]
    la = _log_decay(proj[:, C_G:IN_PAD], wgu_ref, bg_ref)
    b = _seg_cumsum(la, CHUNK)
    b3 = b.reshape(N_CHUNKS, CHUNK, GLA_KDIM)
    bl = b3[:, CHUNK - 1:CHUNK, :]
    blb = jnp.broadcast_to(bl, (N_CHUNKS, CHUNK, GLA_KDIM)).reshape(TN, GLA_KDIM)
    q_t = ((q * (GLA_HEAD_K ** -0.5)) * jnp.exp(b)).astype(BF16)
    k_t = (k * jnp.exp(-b)).astype(BF16)
    k_end = (k * jnp.exp(blb - b)).astype(BF16)
    dec = jnp.exp(bl)
    yield

    ri = lax.broadcasted_iota(I32, (TN, TN), 0)
    ci = lax.broadcasted_iota(I32, (TN, TN), 1)
    causal = (ri // CHUNK == ci // CHUNK) & (ci <= ri)
    lane_k = lax.broadcasted_iota(I32, (TN, LANES), 1) // GLA_HEAD_K

    o_heads = [None] * GLA_HEADS
    for p in range(N_PAIRS):
        ks = slice(LANES * p, LANES * (p + 1))
        vs = slice(2 * GLA_HEAD_V * p, 2 * GLA_HEAD_V * (p + 1))
        qp = q_t[:, ks]
        kp = k_t[:, ks]
        o_inter = []
        st = st_ref[p]
        for c in range(N_CHUNKS):
            rs = slice(CHUNK * c, CHUNK * (c + 1))
            o_inter.append(_dot_nt(qp[rs], st.astype(BF16)))
            ds = _dot_tn(v[rs, vs], k_end[rs, ks])
            st = st * dec[c][:, ks] + jnp.where(smask, ds, 0.0)
        st_ref[p] = st
        o_inter = jnp.concatenate(o_inter, axis=0)
        for hh in range(2):
            head = 2 * p + hh
            qm = jnp.where(lane_k == hh, qp, jnp.zeros_like(qp))
            sc = jnp.where(causal, _dot_nt(qm, kp), 0.0).astype(BF16)
            hv = slice(GLA_HEAD_V * head, GLA_HEAD_V * (head + 1))
            o_heads[head] = _dot(sc, v[:, hv]) + o_inter[:, GLA_HEAD_V * hh:GLA_HEAD_V * (hh + 1)]

    gnw = gnw_ref[...]
    y_heads = []
    for head in range(GLA_HEADS):
        o = o_heads[head]
        y_heads.append(o * lax.rsqrt(jnp.mean(o * o, axis=-1, keepdims=True) + EPS) * gnw)
    y_gla = jnp.concatenate(y_heads, axis=1) * (r * jax.nn.sigmoid(r))
    yield

    ycat = jnp.concatenate([y_pool, y_gla], axis=1).astype(BF16)
    h = x + _dot(ycat, wout_ref[...])
    yield

    xn = _rms(h, nfw_ref[...])
    xh = xn.astype(BF16)
    xl = (xn - xh.astype(F32)).astype(BF16)
    ph = _dot_nt(wr_ref[...], xh)
    plo = _dot_nt(wr_ref[...], xl)
    lt = ((ph[0:ROUTE_ROWS] + ph[ROUTE_ROWS:2 * ROUTE_ROWS])
          + (plo[0:ROUTE_ROWS] + plo[ROUTE_ROWS:2 * ROUTE_ROWS])) + br_ref[...]
    route = _route(lt)
    route_out[...] = route
    _store_records(rec_out, h, jnp.concatenate([route, jnp.zeros((LANES - 8, TN), F32)], axis=0).T)


def _mixer_kernel(x_ref, meta_ref, nmw_ref, win_hbm, wgu_ref, bg_ref, gnw_ref, pw_ref, ps_ref,
                  wout_hbm, nfw_ref, wr_ref, br_ref,
                  hext_ref, route_ref, win_ref, wout_ref, pvh_ref, st_ref, winf_ref, woutf_ref, wsem):
    bi = pl.program_id(0)
    t = pl.program_id(1)

    @pl.when((bi == 0) & (t == 0))
    def _():
        in_copy = pltpu.make_async_copy(win_hbm.at[0], winf_ref, wsem.at[0])
        out_copy = pltpu.make_async_copy(wout_hbm.at[0], woutf_ref, wsem.at[1])
        in_copy.start()
        out_copy.start()
        in_copy.wait()
        win_ref[:, 0:C_G] = winf_ref[:, 0:C_G].astype(BF16)
        win_ref[:, C_G:IN_PAD] = jnp.zeros((D_MODEL, LANES), BF16)
        win_ref[:, C_G:C_G + GATE_RANK] = winf_ref[:, C_G:IN_COLS].astype(BF16)
        out_copy.wait()
        wout_ref[...] = woutf_ref[...].astype(BF16)

    @pl.when(t == 0)
    def _():
        smask = _state_mask()
        um = _rms(meta_ref[...], nmw_ref[...]).astype(BF16)
        pm = _dot(um, win_ref[...])
        la = _log_decay(pm[:, C_G:IN_PAD], wgu_ref, bg_ref)
        b = _seg_cumsum(la, N_META)
        k_end = (pm[:, C_K:C_V] * jnp.exp(b[N_META - 1:N_META, :] - b)).astype(BF16)
        vm = pm[:, C_V:C_R].astype(BF16)
        for p in range(N_PAIRS):
            ds = _dot_tn(vm[:, 2 * GLA_HEAD_V * p:2 * GLA_HEAD_V * (p + 1)],
                         k_end[:, LANES * p:LANES * (p + 1)])
            ds = jnp.where(smask, ds, 0.0)
            for s in range(SEQS):
                st_ref[s, p] = ds
        for s in range(SEQS):
            pvh_ref[s, 0:HIST, :] = pm[:, 0:POOL_WIDTH]

    tiles = [
        _mixer_tile(x_ref[s], pvh_ref.at[s], st_ref.at[s], nmw_ref, win_ref, wgu_ref, bg_ref, gnw_ref,
                    pw_ref, ps_ref, wout_ref, nfw_ref, wr_ref, br_ref,
                    hext_ref.at[s], route_ref.at[s])
        for s in range(SEQS)]
    _run_staggered(tiles, MIXER_LAG)


def _mixer_call(x, meta, nmw, winf, wgu, bg, gnw, pw, ps, woutf, nfw, wr, br):
    bsz, seq, _ = x.shape

    def full(shape):
        return pl.BlockSpec(shape, lambda b, t: (0,) * len(shape))

    return pl.pallas_call(
        _mixer_kernel,
        grid=(bsz // SEQS, seq // TN),
        in_specs=[
            pl.BlockSpec((SEQS, TN, D_MODEL), lambda b, t: (b, t, 0)),
            full((N_META, D_MODEL)),
            full((1, D_MODEL)),
            pl.BlockSpec(memory_space=pl.ANY),
            full((LANES, GLA_KDIM)),
            full((1, GLA_KDIM)),
            full((1, GLA_HEAD_V)),
            full((len(POOL_WINDOWS), POOL_GROUP, POOL_GROUP)),
            full((1, POOL_WIDTH)),
            pl.BlockSpec(memory_space=pl.ANY),
            full((1, D_MODEL)),
            full((2 * ROUTE_ROWS, D_MODEL)),
            full((ROUTE_ROWS, TN)),
        ],
        out_specs=[
            pl.BlockSpec((SEQS, TN * REC, LANES), lambda b, t: (b, t, 0)),
            pl.BlockSpec((SEQS, 8, TN), lambda b, t: (b, 0, t)),
        ],
        out_shape=[
            jax.ShapeDtypeStruct((bsz, seq * REC, LANES), F32),
            jax.ShapeDtypeStruct((bsz, 8, seq), F32),
        ],
        scratch_shapes=[
            pltpu.VMEM((D_MODEL, IN_PAD), BF16),
            pltpu.VMEM((D_MODEL, D_MODEL), BF16),
            pltpu.VMEM((SEQS, HIST + TN, POOL_WIDTH), F32),
            pltpu.VMEM((SEQS, N_PAIRS, 2 * GLA_HEAD_V, 2 * GLA_HEAD_K), F32),
            pltpu.VMEM((D_MODEL, IN_COLS), F32),
            pltpu.VMEM((D_MODEL, D_MODEL), F32),
            pltpu.SemaphoreType.DMA((2,)),
        ],
        compiler_params=pltpu.CompilerParams(
            dimension_semantics=("arbitrary", "arbitrary"),
            vmem_limit_bytes=VMEM_LIMIT),
        name="mixer",
    )(x, meta, nmw, winf, wgu, bg, gnw, pw, ps, woutf, nfw, wr, br)


def _copy_records(slot_ref, tok_ref, slot_hbm, sem, to_slots):
    def body(gi, carry):
        for j in range(ISSUE_UNROLL):
            r = gi * ISSUE_UNROLL + j
            tok_rec = tok_ref.at[pl.ds(r * REC, REC), :]
            slot_rec = slot_hbm.at[pl.ds(slot_ref[0, r] * REC, REC), :]
            src, dst = (tok_rec, slot_rec) if to_slots else (slot_rec, tok_rec)
            pltpu.make_async_copy(src, dst, sem).start(priority=j % 2)
        return carry
    lax.fori_loop(0, slot_ref.shape[-1] // ISSUE_UNROLL, body, 0)
    pltpu.make_async_copy(tok_ref, tok_ref, sem).wait()


def _dispatch_kernel(zoff_ref, zflag_ref, slot_ref, tok_ref, rows_hbm, zeros_ref, sem, zsem):
    @pl.when(pl.program_id(0) == 0)
    def _():
        zeros_ref[...] = jnp.zeros(zeros_ref.shape, zeros_ref.dtype)
        for b in range(zflag_ref.shape[0]):
            @pl.when(zflag_ref[b] != 0)
            def _(b=b):
                off = pl.multiple_of(zoff_ref[b], TM * REC)
                pltpu.make_async_copy(zeros_ref, rows_hbm.at[pl.ds(off, TM * REC), :], zsem).start()
        for b in range(zflag_ref.shape[0]):
            @pl.when(zflag_ref[b] != 0)
            def _(b=b):
                off = pl.multiple_of(zoff_ref[b], TM * REC)
                pltpu.make_async_copy(zeros_ref, rows_hbm.at[pl.ds(off, TM * REC), :], zsem).wait()

    _copy_records(slot_ref, tok_ref, rows_hbm, sem, to_slots=True)


def _combine_kernel(slot_ref, rows_hbm, out_ref, stage_ref, sem):
    _copy_records(slot_ref, stage_ref, rows_hbm, sem, to_slots=False)
    for j in range(D_TILES):
        out_ref[:, LANES * j:LANES * (j + 1)] = stage_ref[pl.ds(j, PB_COMBINE, stride=REC), :]


def _slot_block_spec(index_map, pb):
    return pl.BlockSpec((None, 1, pb), index_map, memory_space=pltpu.SMEM)


def _dispatch_call(zoff, zflag, slots, recs, n_slots):
    n_tokens = recs.shape[0] // REC
    grid_spec = pltpu.PrefetchScalarGridSpec(
        num_scalar_prefetch=2,
        grid=(n_tokens // PB_DISPATCH,),
        in_specs=[
            _slot_block_spec(lambda i, zo, zf: (i, 0, 0), PB_DISPATCH),
            pl.BlockSpec((PB_DISPATCH * REC, LANES), lambda i, zo, zf: (i, 0)),
        ],
        out_specs=pl.BlockSpec(memory_space=pl.ANY),
        scratch_shapes=[
            pltpu.VMEM((TM * REC, LANES), F32),
            pltpu.SemaphoreType.DMA(()),
            pltpu.SemaphoreType.DMA(()),
        ],
    )
    return pl.pallas_call(
        _dispatch_kernel,
        grid_spec=grid_spec,
        out_shape=jax.ShapeDtypeStruct((n_slots * REC, LANES), F32),
        compiler_params=pltpu.CompilerParams(dimension_semantics=("arbitrary",),
                                             vmem_limit_bytes=VMEM_LIMIT),
        name="dispatch",
    )(zoff, zflag, slots, recs)


def _combine_call(slots, res, n_tokens):
    return pl.pallas_call(
        _combine_kernel,
        grid=(n_tokens // PB_COMBINE,),
        in_specs=[
            _slot_block_spec(lambda i: (i, 0, 0), PB_COMBINE),
            pl.BlockSpec(memory_space=pl.ANY),
        ],
        out_specs=pl.BlockSpec((PB_COMBINE, D_MODEL), lambda i: (i, 0)),
        out_shape=jax.ShapeDtypeStruct((n_tokens, D_MODEL), F32),
        scratch_shapes=[pltpu.VMEM((PB_COMBINE * REC, LANES), F32), pltpu.SemaphoreType.DMA(())],
        compiler_params=pltpu.CompilerParams(dimension_semantics=("arbitrary",),
                                             vmem_limit_bytes=VMEM_LIMIT),
        name="combine",
    )(slots, res)


def _moe_kernel(ea_ref, eb_ref, na_ref,
                rows_ref, wga_ref, wua_ref, wda_ref, wgb_ref, wub_ref, wdb_ref,
                nfw_ref, fnw_ref,
                out_ref, wgu_a, wd_a, wgu_b, wd_b):
    i = pl.program_id(0)
    n_act = na_ref[0]
    prev = jnp.maximum(i - 1, 0)

    @pl.when((i == 0) | (ea_ref[i] != ea_ref[prev]))
    def _():
        wgu_a[:, 0:EXPERT_HIDDEN] = wga_ref[...].astype(BF16)
        wgu_a[:, EXPERT_HIDDEN:2 * EXPERT_HIDDEN] = wua_ref[...].astype(BF16)
        wd_a[...] = wda_ref[...].astype(BF16)

    @pl.when((i == 0) | (eb_ref[i] != eb_ref[prev]))
    def _():
        wgu_b[:, 0:EXPERT_HIDDEN] = wgb_ref[...].astype(BF16)
        wgu_b[:, EXPERT_HIDDEN:2 * EXPERT_HIDDEN] = wub_ref[...].astype(BF16)
        wd_b[...] = wdb_ref[...].astype(BF16)

    def part(first, n):
        hrow, slab = _load_records(rows_ref, n, first)
        wa = slab[:, 3:4]
        wb = slab[:, 4:5]
        xn = _rms(hrow, nfw_ref[...]).astype(BF16)
        yield
        y = None
        for wgt, wgu, wd in ((wa, wgu_a, wd_a), (wb, wgu_b, wd_b)):
            gu = _dot(xn, wgu[...])
            yield
            gate = gu[:, 0:EXPERT_HIDDEN]
            hid = ((gate * jax.nn.sigmoid(gate)) * gu[:, EXPERT_HIDDEN:2 * EXPERT_HIDDEN]).astype(BF16)
            yield
            ye = wgt * _dot(hid, wd[...])
            y = ye if y is None else y + ye
            yield
        _store_records(out_ref, _rms(hrow + y, fnw_ref[...]), jnp.zeros((n, LANES), F32), first)

    @pl.when(i < n_act)
    def _():
        _run_staggered([part(k * (TM // MOE_PARTS), TM // MOE_PARTS) for k in range(MOE_PARTS)])

    @pl.when(i >= n_act)
    def _():
        out_ref[...] = jnp.zeros(out_ref.shape, out_ref.dtype)


def _moe_call(ea, eb, nact, rows, wg, wu, wd, nfw, fnw):
    n_tiles = rows.shape[0] // (TM * REC)
    sel_a = lambda i, ea, eb, na: (ea[i], 0, 0)
    sel_b = lambda i, ea, eb, na: (eb[i], 0, 0)
    tile = lambda i, ea, eb, na: (jnp.minimum(i, na[0] - 1), 0)
    up_spec = lambda sel: pl.BlockSpec((None, D_MODEL, EXPERT_HIDDEN), sel)
    down_spec = lambda sel: pl.BlockSpec((None, EXPERT_HIDDEN, D_MODEL), sel)
    grid_spec = pltpu.PrefetchScalarGridSpec(
        num_scalar_prefetch=3,
        grid=(n_tiles,),
        in_specs=[
            pl.BlockSpec((TM * REC, LANES), tile),
            up_spec(sel_a), up_spec(sel_a), down_spec(sel_a),
            up_spec(sel_b), up_spec(sel_b), down_spec(sel_b),
            pl.BlockSpec((1, D_MODEL), lambda i, ea, eb, na: (0, 0)),
            pl.BlockSpec((1, D_MODEL), lambda i, ea, eb, na: (0, 0)),
        ],
        out_specs=pl.BlockSpec((TM * REC, LANES), lambda i, ea, eb, na: (i, 0)),
        scratch_shapes=[
            pltpu.VMEM((D_MODEL, 2 * EXPERT_HIDDEN), BF16),
            pltpu.VMEM((EXPERT_HIDDEN, D_MODEL), BF16),
            pltpu.VMEM((D_MODEL, 2 * EXPERT_HIDDEN), BF16),
            pltpu.VMEM((EXPERT_HIDDEN, D_MODEL), BF16),
        ],
    )
    return pl.pallas_call(
        _moe_kernel,
        grid_spec=grid_spec,
        out_shape=jax.ShapeDtypeStruct((n_tiles * TM * REC, LANES), F32),
        compiler_params=pltpu.CompilerParams(
            dimension_semantics=("arbitrary",),
            vmem_limit_bytes=VMEM_LIMIT),
        name="moe",
    )(ea, eb, nact, rows, wg, wu, wd, wg, wu, wd, nfw, fnw)


def _slot_tables(route, n_tokens):
    g = route[:, 0, :].reshape(n_tokens).astype(I32)
    ja = route[:, 1, :].reshape(n_tokens).astype(I32)
    jb = route[:, 2, :].reshape(n_tokens).astype(I32)
    pair = (ja * (7 - ja)) // 2 + (jb - ja - 1)
    bucket = g * N_PAIR_BUCKETS + pair
    onehot = (bucket[None, :] == jnp.arange(N_BUCKETS, dtype=I32)[:, None]).astype(I32)
    csum = jnp.cumsum(onehot, axis=1)
    counts = csum[:, -1]
    rank = jnp.sum(csum * onehot, axis=0) - 1
    padded = ((counts + TM - 1) // TM) * TM
    ends = jnp.cumsum(padded)
    starts = ends - padded
    dst = jnp.sum(onehot * starts[:, None], axis=0) + rank
    n_tiles = n_tokens // TM + N_BUCKETS
    tile_start = jnp.arange(n_tiles, dtype=I32) * TM
    total = ends[-1]
    nact = total // TM
    tb = jnp.minimum(jnp.sum((tile_start[:, None] >= ends[None, :]).astype(I32), axis=1), N_BUCKETS - 1)
    tb = jnp.where(tile_start < total, tb, jnp.take(tb, jnp.maximum(nact - 1, 0)))
    pair_a = jnp.array([0, 0, 0, 1, 1, 2], I32)
    pair_b = jnp.array([1, 2, 3, 2, 3, 3], I32)
    ea = (tb // N_PAIR_BUCKETS) * EXPERTS_PER_GROUP + jnp.take(pair_a, tb % N_PAIR_BUCKETS)
    eb = (tb // N_PAIR_BUCKETS) * EXPERTS_PER_GROUP + jnp.take(pair_b, tb % N_PAIR_BUCKETS)
    tail = nact + jnp.arange(N_BUCKETS, dtype=I32)
    zflag = jnp.concatenate([(counts % TM != 0).astype(I32), (tail < n_tiles).astype(I32)])
    zoff = jnp.concatenate([jnp.maximum(ends - TM, 0), jnp.minimum(tail, n_tiles - 1) * TM]) * REC
    return ea, eb, jnp.reshape(nact, (1,)).astype(I32), zoff.astype(I32), zflag, dst


def kernel(x, meta_tokens, norm_mix_w, w_in, w_gate_up, b_gate, gla_norm_w, pool_w, pool_scale, w_out, norm_ffn_w, w_router_group, b_router_group, w_router_expert, b_router_expert, w_expert_gate, w_expert_up, w_expert_down, final_norm_w):
    bsz, seq, d = x.shape
    assert d == D_MODEL and seq % TN == 0 and bsz % SEQS == 0 and (bsz * seq) % TM == 0
    assert norm_mix_w.shape[0] == 1
    n_tokens = bsz * seq

    wgu_gate = jnp.pad(w_gate_up[0], ((0, LANES - GATE_RANK), (0, 0))).astype(BF16)
    w_r = jnp.pad(jnp.concatenate([w_router_group[0], w_router_expert[0]], axis=1).T,
                  ((0, ROUTE_ROWS - N_GROUPS - N_EXPERTS), (0, 0)))
    w_r_hi = w_r.astype(BF16)
    w_r_lo = (w_r - w_r_hi.astype(F32)).astype(BF16)
    b_r = jnp.pad(jnp.concatenate([b_router_group[0], b_router_expert[0]]),
                  (0, ROUTE_ROWS - N_GROUPS - N_EXPERTS))
    b_r = jnp.broadcast_to(b_r[:, None], (ROUTE_ROWS, TN))

    hext, route = _mixer_call(
        x, meta_tokens, norm_mix_w[0].reshape(1, d), w_in, wgu_gate, b_gate[0].reshape(1, GLA_KDIM),
        gla_norm_w[0].reshape(1, GLA_HEAD_V), pool_w[0].astype(BF16), pool_scale[0].reshape(1, POOL_WIDTH),
        w_out, norm_ffn_w[0].reshape(1, d), jnp.concatenate([w_r_hi, w_r_lo], axis=0), b_r)

    ea, eb, nact, zoff, zflag, slots = _slot_tables(route, n_tokens)
    n_slots = n_tokens + N_BUCKETS * TM
    rows = _dispatch_call(zoff, zflag, slots.reshape(n_tokens // PB_DISPATCH, 1, PB_DISPATCH),
                          hext.reshape(n_tokens * REC, LANES), n_slots)
    res = _moe_call(ea, eb, nact, rows, w_expert_gate[0], w_expert_up[0], w_expert_down[0],
                    norm_ffn_w[0].reshape(1, d), final_norm_w.reshape(1, d))
    out = _combine_call(slots.reshape(n_tokens // PB_COMBINE, 1, PB_COMBINE), res, n_tokens)
    return out.reshape(bsz, seq, d)
```
